```python
import math
import jax
import jax.numpy as jnp
from jax import lax
import numpy as np

D_MODEL = 2048
BATCH = 4
SEQ = 8192
DEPTH = 1

CTX_LEN = 256
GRID_W = 64
SSM_EXPAND = 2
D_INNER = SSM_EXPAND * D_MODEL
SSM_HEAD_DIM = 64
SSM_HEADS = D_INNER // SSM_HEAD_DIM
SSM_GROUPS = 8
HEADS_PER_GROUP = SSM_HEADS // SSM_GROUPS
SSM_STATE = 128
SSM_CONV = 5
CHUNK = 128
CONV_DIM = D_INNER + 2 * SSM_GROUPS * SSM_STATE
D_CF = D_MODEL
CF_KERNEL = 31
D_FF = -(-8 * D_MODEL // (3 * 256)) * 256
EPS = 1e-6
OFF_Z = D_INNER
OFF_XBC = OFF_Z + CONV_DIM
OFF_DT = OFF_XBC + 2 * SSM_HEADS
OFF_GLU = OFF_DT + 2 * D_CF
D_IN_TOTAL = OFF_GLU + 2 * D_MODEL

kernel_name = "hybrid_ssd_conformer_dit_block"


def rmsnorm(x, w):
    xf = x.astype(jnp.float32)
    y = xf * lax.rsqrt(jnp.mean(xf * xf, axis=-1, keepdims=True) + EPS)
    return (y * w.astype(jnp.float32)).astype(x.dtype)


def layernorm(x, g, b):
    xf = x.astype(jnp.float32)
    mu = jnp.mean(xf, axis=-1, keepdims=True)
    xc = xf - mu
    var = jnp.mean(xc * xc, axis=-1, keepdims=True)
    y = xc * lax.rsqrt(var + EPS) * g.astype(jnp.float32) + b.astype(jnp.float32)
    return y.astype(x.dtype)


def ada_params(cvec, w_mod, b_mod):
    m = jax.nn.silu(cvec) @ w_mod + b_mod
    return jnp.split(m, 6, axis=-1)


def modulate(h, shift, scale):
    return h * (1 + scale) + shift


def dwconv_seq(u, w, bias):
    k = w.shape[0]
    out = lax.conv_general_dilated(
        u, w[:, None, :].astype(u.dtype), window_strides=(1,), padding=[(k // 2, k // 2)],
        dimension_numbers=("NWC", "WIO", "NWC"), feature_group_count=u.shape[-1])
    return out + bias.astype(u.dtype)


def dwconv_grid_columns(u, w, bias, rows):
    b, L, C = u.shape
    k = w.shape[0]
    ug = u.reshape(b, rows, GRID_W, C)
    out = lax.conv_general_dilated(
        ug, w[:, None, None, :].astype(u.dtype), window_strides=(1, 1),
        padding=[(k // 2, k // 2), (0, 0)],
        dimension_numbers=("NHWC", "HWIO", "NHWC"), feature_group_count=C)
    return out.reshape(b, L, C) + bias.astype(u.dtype)


def ssd_chunked(xs, bm, cm, dt, a, h0):
    b, L = xs.shape[:2]
    nc = L // CHUNK

    def to_chunks(t):
        return jnp.moveaxis(t.reshape(b, nc, CHUNK, *t.shape[2:]), 1, 0)

    xdt = (xs.astype(jnp.float32) * dt[..., None]).reshape(
        b, L, SSM_GROUPS, HEADS_PER_GROUP, SSM_HEAD_DIM)
    da = (dt * a).reshape(b, L, SSM_GROUPS, HEADS_PER_GROUP)
    lower = jnp.tril(jnp.ones((CHUNK, CHUNK), dtype=bool))[None, :, :, None, None]

    def step(h, inp):
        xdt_c, b_c, c_c, da_c = inp
        cs = jnp.cumsum(da_c, axis=1)
        seg = cs[:, :, None] - cs[:, None, :]
        decay = jnp.exp(jnp.where(lower, seg, -jnp.inf))
        cb = jnp.einsum("blgn,bsgn->blsg", c_c, b_c)
        y_diag = jnp.einsum("blsgj,bsgjp->blgjp", cb[..., None] * decay, xdt_c)
        y_off = jnp.einsum("blgn,bgjpn->blgjp", c_c, h) * jnp.exp(cs)[..., None]
        to_end = jnp.exp(cs[:, -1:] - cs)
        h_new = h * jnp.exp(cs[:, -1])[..., None, None] + jnp.einsum(
            "bsgn,bsgjp->bgjpn", b_c, xdt_c * to_end[..., None])
        return h_new, y_diag + y_off

    h_last, ys = lax.scan(step, h0, (to_chunks(xdt), to_chunks(bm.astype(jnp.float32)),
                                     to_chunks(cm.astype(jnp.float32)), to_chunks(da)))
    y = jnp.moveaxis(ys, 0, 1).reshape(b, L, SSM_HEADS, SSM_HEAD_DIM)
    return y, h_last


def flip_seq(t):
    return jnp.flip(t, axis=1)


def mixer_front(h, w_in, conv_w, conv_b, dt_bias, a_log, h0_f, h0_b):
    b, L, _ = h.shape
    proj = h @ w_in
    z, xbc, dt_raw, glu, gates = jnp.split(proj, [OFF_Z, OFF_XBC, OFF_DT, OFF_GLU], axis=-1)
    xbc = jax.nn.silu(dwconv_seq(xbc, conv_w, conv_b))
    xs, bm, cm = jnp.split(xbc, [D_INNER, D_INNER + SSM_GROUPS * SSM_STATE], axis=-1)
    xs = xs.reshape(b, L, SSM_HEADS, SSM_HEAD_DIM)
    bm = bm.reshape(b, L, SSM_GROUPS, SSM_STATE)
    cm = cm.reshape(b, L, SSM_GROUPS, SSM_STATE)
    dt = jax.nn.softplus(dt_raw.astype(jnp.float32).reshape(b, L, 2, SSM_HEADS)
                         + dt_bias.astype(jnp.float32))
    a = -jnp.exp(a_log.astype(jnp.float32))
    y_f, h_f = ssd_chunked(xs, bm, cm, dt[:, :, 0], a[0], h0_f)
    y_b, h_b = ssd_chunked(flip_seq(xs), flip_seq(bm), flip_seq(cm), flip_seq(dt[:, :, 1]), a[1], h0_b)
    y = y_f + flip_seq(y_b)
    return (z, xs, y, glu, gates), h_f, h_b


def mixer_back(parts, d_skip, ssm_norm, cf_conv, cf_ln_g, cf_ln_b, w_proj_a, w_proj_b, w_out):
    z, xs, y, glu, gates = parts
    b, L = z.shape[:2]
    y = (y + d_skip.astype(jnp.float32)[:, None] * xs.astype(jnp.float32)).reshape(
        b, L, D_INNER).astype(z.dtype)
    y_a = rmsnorm(y * jax.nn.silu(z), ssm_norm) @ w_proj_a
    u, v = jnp.split(glu, 2, axis=-1)
    cf = jax.nn.silu(layernorm(cf_conv(u * jax.nn.sigmoid(v)), cf_ln_g, cf_ln_b))
    y_b = cf @ w_proj_b
    g_a, g_b = jnp.split(gates, 2, axis=-1)
    return (jax.nn.sigmoid(g_a) * y_a + jax.nn.sigmoid(g_b) * y_b) @ w_out


def swiglu(h, w_gate, w_up, w_down):
    return (jax.nn.silu(h @ w_gate) * (h @ w_up)) @ w_down


def setup_inputs(seed: int = 0) -> dict:
    key = jax.random.key(seed)
    ks = jax.random.split(key, 26)

    def nrm(k, shape, std):
        return jax.random.normal(k, shape, jnp.float32) * std

    x = nrm(ks[0], (BATCH, SEQ, D_MODEL), 1.0)
    c = nrm(ks[1], (BATCH, D_MODEL), 1.0)
    ctx = nrm(ks[2], (BATCH, CTX_LEN, D_MODEL), 1.0)
    c_ctx = nrm(ks[3], (D_MODEL,), 1.0)
    w_mod = nrm(ks[4], (DEPTH, D_MODEL, 6 * D_MODEL), 0.5 * D_MODEL ** -0.5)
    b_mod = nrm(ks[5], (DEPTH, 6 * D_MODEL), 0.02)
    norm_mix = 1.0 + nrm(ks[6], (DEPTH, D_MODEL), 0.02)
    w_in = nrm(ks[7], (DEPTH, D_MODEL, D_IN_TOTAL), D_MODEL ** -0.5)
    ssm_conv_w = nrm(ks[8], (DEPTH, SSM_CONV, CONV_DIM), SSM_CONV ** -0.5)
    ssm_conv_b = nrm(ks[9], (DEPTH, CONV_DIM), 0.02)
    dt0 = jnp.exp(jax.random.uniform(ks[10], (DEPTH, 2, SSM_HEADS), jnp.float32,
                                     minval=math.log(1e-3), maxval=math.log(1e-1)))
    dt_bias = dt0 + jnp.log(-jnp.expm1(-dt0))
    a_log = jnp.log(jax.random.uniform(ks[11], (DEPTH, 2, SSM_HEADS), jnp.float32,
                                       minval=1.0, maxval=16.0))
    d_skip = 1.0 + nrm(ks[12], (DEPTH, SSM_HEADS), 0.1)
    ssm_norm = 1.0 + nrm(ks[13], (DEPTH, D_INNER), 0.02)
    cf_conv_w = nrm(ks[14], (DEPTH, CF_KERNEL, D_CF), CF_KERNEL ** -0.5)
    cf_conv_b = nrm(ks[15], (DEPTH, D_CF), 0.02)
    cf_ln_g = 1.0 + nrm(ks[16], (DEPTH, D_CF), 0.02)
    cf_ln_b = nrm(ks[17], (DEPTH, D_CF), 0.02)
    w_proj_a = nrm(ks[18], (DEPTH, D_INNER, D_MODEL), D_INNER ** -0.5)
    w_proj_b = nrm(ks[19], (DEPTH, D_CF, D_MODEL), D_CF ** -0.5)
    w_out = nrm(ks[20], (DEPTH, D_MODEL, D_MODEL), D_MODEL ** -0.5)
    norm_ffn = 1.0 + nrm(ks[21], (DEPTH, D_MODEL), 0.02)
    w_ffn_gate = nrm(ks[22], (DEPTH, D_MODEL, D_FF), D_MODEL ** -0.5)
    w_ffn_up = nrm(ks[23], (DEPTH, D_MODEL, D_FF), D_MODEL ** -0.5)
    w_ffn_down = nrm(ks[24], (DEPTH, D_FF, D_MODEL), D_FF ** -0.5)
    norm_final = 1.0 + nrm(ks[25], (D_MODEL,), 0.02)
    return {"x": x, "c": c, "ctx": ctx, "c_ctx": c_ctx, "w_mod": w_mod, "b_mod": b_mod,
            "norm_mix": norm_mix, "w_in": w_in, "ssm_conv_w": ssm_conv_w, "ssm_conv_b": ssm_conv_b,
            "dt_bias": dt_bias, "a_log": a_log, "d_skip": d_skip, "ssm_norm": ssm_norm,
            "cf_conv_w": cf_conv_w, "cf_conv_b": cf_conv_b, "cf_ln_g": cf_ln_g, "cf_ln_b": cf_ln_b,
            "w_proj_a": w_proj_a, "w_proj_b": w_proj_b, "w_out": w_out, "norm_ffn": norm_ffn,
            "w_ffn_gate": w_ffn_gate, "w_ffn_up": w_ffn_up, "w_ffn_down": w_ffn_down,
            "norm_final": norm_final}


def reference(x, c, ctx, c_ctx, w_mod, b_mod, norm_mix, w_in, ssm_conv_w, ssm_conv_b, dt_bias,
              a_log, d_skip, ssm_norm, cf_conv_w, cf_conv_b, cf_ln_g, cf_ln_b, w_proj_a, w_proj_b,
              w_out, norm_ffn, w_ffn_gate, w_ffn_up, w_ffn_down, norm_final):
    b = x.shape[0]
    rows = x.shape[1] // GRID_W
    zero_state = jnp.zeros((b, SSM_GROUPS, HEADS_PER_GROUP, SSM_HEAD_DIM, SSM_STATE), jnp.float32)
    for l in range(DEPTH):
        sh1, sc1, g1, sh2, sc2, g2 = [t[:, None, :] for t in ada_params(c, w_mod[l], b_mod[l])]
        csh1, csc1, cg1, csh2, csc2, cg2 = ada_params(c_ctx, w_mod[l], b_mod[l])

        hc = modulate(rmsnorm(ctx, norm_mix[l]), csh1, csc1)
        ctx_parts, h_f, h_b = mixer_front(hc, w_in[l], ssm_conv_w[l], ssm_conv_b[l], dt_bias[l],
                                          a_log[l], zero_state, zero_state)

        hx = modulate(rmsnorm(x, norm_mix[l]), sh1, sc1)
        x_parts, _, _ = mixer_front(hx, w_in[l], ssm_conv_w[l], ssm_conv_b[l], dt_bias[l],
                                    a_log[l], h_f, h_b)
        x = x + g1 * mixer_back(
            x_parts, d_skip[l], ssm_norm[l],
            lambda t: dwconv_grid_columns(t, cf_conv_w[l], cf_conv_b[l], rows),
            cf_ln_g[l], cf_ln_b[l], w_proj_a[l], w_proj_b[l], w_out[l])
        hx = modulate(rmsnorm(x, norm_ffn[l]), sh2, sc2)
        x = x + g2 * swiglu(hx, w_ffn_gate[l], w_ffn_up[l], w_ffn_down[l])

        if l + 1 < DEPTH:
            ctx = ctx + cg1 * mixer_back(
                ctx_parts, d_skip[l], ssm_norm[l],
                lambda t: dwconv_seq(t, cf_conv_w[l], cf_conv_b[l]),
                cf_ln_g[l], cf_ln_b[l], w_proj_a[l], w_proj_b[l], w_out[l])
            hc = modulate(rmsnorm(ctx, norm_ffn[l]), csh2, csc2)
            ctx = ctx + cg2 * swiglu(hc, w_ffn_gate[l], w_ffn_up[l], w_ffn_down[l])
    return rmsnorm(x, norm_final)
```

```python
import functools

import jax
import jax.numpy as jnp
import numpy as np
from jax import lax
from jax.experimental import pallas as pl
from jax.experimental.pallas import tpu as pltpu

F32 = jnp.float32
BF16 = jnp.bfloat16

D_MODEL = 2048
GRID_W = 64
D_INNER = 2 * D_MODEL
HEAD_DIM = 64
N_HEADS = D_INNER // HEAD_DIM
N_GROUPS = 8
GROUP_W = D_INNER // N_GROUPS
D_STATE = 128
SSM_CONV = 5
CHUNK = 128
BC_DIM = N_GROUPS * D_STATE
CONV_DIM = D_INNER + 2 * BC_DIM
CF_KERNEL = 31
D_FF = 5632
EPS = 1e-6
OFF_Z = D_INNER
OFF_XBC = OFF_Z + CONV_DIM
OFF_DT = OFF_XBC + 2 * N_HEADS
OFF_GLU = OFF_DT + 2 * D_MODEL

LANES = 128
HALF = LANES // 2
VMEM_LIMIT = 56 * 1024 * 1024
NEG_BIG = -1e30


def _cparams(*sem):
    return pltpu.CompilerParams(dimension_semantics=sem, vmem_limit_bytes=VMEM_LIMIT)


def _sigmoid(v):
    return jax.nn.sigmoid(v)


def _silu(v):
    return v * jax.nn.sigmoid(v)


def _rms(v, w):
    return v * lax.rsqrt(jnp.mean(v * v, axis=-1, keepdims=True) + EPS) * w


def _ada_kernel(c_ref, w_ref, b_ref, o_ref):
    s = _silu(c_ref[...]).astype(BF16)
    o_ref[...] = jnp.dot(s, w_ref[...].astype(BF16), preferred_element_type=F32) + b_ref[...]


def _ada(cc, w_mod, b_mod):
    n = w_mod.shape[1]
    tn = 1024
    return pl.pallas_call(
        _ada_kernel,
        grid=(n // tn,),
        in_specs=[pl.BlockSpec((8, D_MODEL), lambda j: (0, 0)),
                  pl.BlockSpec((D_MODEL, tn), lambda j: (0, j)),
                  pl.BlockSpec((1, tn), lambda j: (0, j))],
        out_specs=pl.BlockSpec((8, tn), lambda j: (0, j)),
        out_shape=jax.ShapeDtypeStruct((8, n), F32),
        compiler_params=_cparams("arbitrary"),
        name="ada_params",
    )(cc, w_mod, b_mod.reshape(1, n))


def _normmod_kernel(x_ref, w_ref, sc_ref, sh_ref, o_ref):
    y = _rms(x_ref[...], w_ref[...])
    o_ref[...] = (y * (1.0 + sc_ref[0]) + sh_ref[0]).astype(o_ref.dtype)


def _normmod(x2d, w, sc, sh, tokens_per_batch, tm=512):
    t = x2d.shape[0]
    tpb = tokens_per_batch // tm
    return pl.pallas_call(
        _normmod_kernel,
        grid=(t // tm,),
        in_specs=[pl.BlockSpec((tm, D_MODEL), lambda i: (i, 0)),
                  pl.BlockSpec((1, D_MODEL), lambda i: (0, 0)),
                  pl.BlockSpec((1, 1, D_MODEL), lambda i: (i // tpb, 0, 0)),
                  pl.BlockSpec((1, 1, D_MODEL), lambda i: (i // tpb, 0, 0))],
        out_specs=pl.BlockSpec((tm, D_MODEL), lambda i: (i, 0)),
        out_shape=jax.ShapeDtypeStruct((t, D_MODEL), BF16),
        compiler_params=_cparams("arbitrary"),
        name="norm_modulate",
    )(x2d, w.reshape(1, D_MODEL), sc, sh)


def _mm_kernel(a_ref, w_ref, o_ref, *, act):
    acc = jnp.dot(a_ref[...], w_ref[...], preferred_element_type=F32)
    if act == "silu":
        acc = _silu(acc)
    elif act == "sigmoid":
        acc = _sigmoid(acc)
    o_ref[...] = acc.astype(o_ref.dtype)


def _mm(a, w, act, out_dtype, tm=512, tn=1024):
    m, k = a.shape
    n = w.shape[1]
    tn = min(tn, n)
    return pl.pallas_call(
        functools.partial(_mm_kernel, act=act),
        grid=(n // tn, m // tm),
        in_specs=[pl.BlockSpec((tm, k), lambda j, i: (i, 0)),
                  pl.BlockSpec((k, tn), lambda j, i: (0, j))],
        out_specs=pl.BlockSpec((tm, tn), lambda j, i: (i, j)),
        out_shape=jax.ShapeDtypeStruct((m, n), out_dtype),
        compiler_params=_cparams("arbitrary", "arbitrary"),
        name="proj_" + act,
    )(a, w)


def _glu_kernel(a_ref, wu_ref, wv_ref, o_ref):
    a = a_ref[...]
    u = jnp.dot(a, wu_ref[...], preferred_element_type=F32)
    v = jnp.dot(a, wv_ref[...], preferred_element_type=F32)
    o_ref[...] = (u * _sigmoid(v)).astype(o_ref.dtype)


def _mm_glu(a, wu, wv, tm=512, tn=512):
    m, k = a.shape
    n = wu.shape[1]
    return pl.pallas_call(
        _glu_kernel,
        grid=(n // tn, m // tm),
        in_specs=[pl.BlockSpec((tm, k), lambda j, i: (i, 0)),
                  pl.BlockSpec((k, tn), lambda j, i: (0, j)),
                  pl.BlockSpec((k, tn), lambda j, i: (0, j))],
        out_specs=pl.BlockSpec((tm, tn), lambda j, i: (i, j)),
        out_shape=jax.ShapeDtypeStruct((m, n), BF16),
        compiler_params=_cparams("arbitrary", "arbitrary"),
        name="proj_glu",
    )(a, wu, wv)


CONV_TQ = 256
CONV_HALO = 16
CONV_CB = 512


def _conv_kernel(xp_ref, xc_ref, xn_ref, dt_ref, cw_ref, cb_ref, dtb_ref, alog_ref, rf_ref, rb_ref,
                 xs_ref, xdf_ref, xdb_ref, b_ref, c_ref, da_ref, win_ref):
    i = pl.program_id(1)
    n = pl.num_programs(1)
    tq = CONV_TQ
    prev = xp_ref[0, 8:16, :].astype(F32)
    nxt = xn_ref[0, 0:8, :].astype(F32)
    win_ref[0:8, :] = jnp.where(i > 0, prev, 0.0)
    win_ref[8:8 + tq, :] = xc_ref[0].astype(F32)
    win_ref[8 + tq:16 + tq, :] = jnp.where(i < n - 1, nxt, 0.0)

    raw = dt_ref[0] + dtb_ref[...]
    dt = jnp.maximum(raw, 0.0) + jnp.log1p(jnp.exp(-jnp.abs(raw)))
    da_ref[0] = dt * (-jnp.exp(alog_ref[...]))
    dtfb = dt[:, 0:LANES]
    hi = dtfb.astype(BF16)
    mid = (dtfb - hi.astype(F32)).astype(BF16)
    parts = jnp.concatenate([hi, mid], axis=1)

    for cblk in range(CONV_DIM // CONV_CB):
        cs = slice(cblk * CONV_CB, (cblk + 1) * CONV_CB)
        acc = jnp.zeros((tq, CONV_CB), F32) + cb_ref[:, cs]
        for k in range(SSM_CONV):
            acc = acc + cw_ref[k:k + 1, cs] * win_ref[pl.ds(6 + k, tq), cs]
        v = _silu(acc)
        if cblk < D_INNER // CONV_CB:
            xs_ref[0, :, cs] = v.astype(xs_ref.dtype)
            ef = jnp.dot(parts, rf_ref[:, cs], preferred_element_type=F32)
            eb = jnp.dot(parts, rb_ref[:, cs], preferred_element_type=F32)
            xdf_ref[0, :, cs] = (v * ef).astype(xdf_ref.dtype)
            xdb_ref[0, :, cs] = (v * eb).astype(xdb_ref.dtype)
        elif cblk < (D_INNER + BC_DIM) // CONV_CB:
            o = cblk * CONV_CB - D_INNER
            b_ref[0, :, o:o + CONV_CB] = v.astype(b_ref.dtype)
        else:
            o = cblk * CONV_CB - D_INNER - BC_DIM
            c_ref[0, :, o:o + CONV_CB] = v.astype(c_ref.dtype)


def _conv(xbc, dt4, cw, cb, dtb4, alog4, rf, rb):
    bsz, seq, _ = xbc.shape
    tq = CONV_TQ
    nt = seq // tq
    hb = tq // CONV_HALO
    nh = seq // CONV_HALO
    out_shapes = (
        jax.ShapeDtypeStruct((bsz, seq, D_INNER), BF16),
        jax.ShapeDtypeStruct((bsz, seq, D_INNER), BF16),
        jax.ShapeDtypeStruct((bsz, seq, D_INNER), BF16),
        jax.ShapeDtypeStruct((bsz, seq, BC_DIM), BF16),
        jax.ShapeDtypeStruct((bsz, seq, BC_DIM), BF16),
        jax.ShapeDtypeStruct((bsz, seq, 2 * LANES), F32),
    )
    tok = lambda w: pl.BlockSpec((1, tq, w), lambda b, i: (b, i, 0))
    full = lambda r, c: pl.BlockSpec((r, c), lambda b, i: (0, 0))
    return pl.pallas_call(
        _conv_kernel,
        grid=(bsz, nt),
        in_specs=[
            pl.BlockSpec((1, CONV_HALO, CONV_DIM), lambda b, i: (b, jnp.maximum(i * hb - 1, 0), 0)),
            tok(CONV_DIM),
            pl.BlockSpec((1, CONV_HALO, CONV_DIM), lambda b, i: (b, jnp.minimum((i + 1) * hb, nh - 1), 0)),
            tok(2 * LANES),
            full(8, CONV_DIM), full(1, CONV_DIM), full(1, 2 * LANES), full(1, 2 * LANES),
            full(2 * LANES, D_INNER), full(2 * LANES, D_INNER),
        ],
        out_specs=(tok(D_INNER), tok(D_INNER), tok(D_INNER), tok(BC_DIM), tok(BC_DIM), tok(2 * LANES)),
        out_shape=out_shapes,
        scratch_shapes=[pltpu.VMEM((tq + 16, CONV_DIM), F32)],
        compiler_params=_cparams("arbitrary", "arbitrary"),
        name="ssm_conv",
    )(xbc, xbc, xbc, dt4, cw, cb, dtb4, alog4, rf, rb)


def _ssd_kernel(xdt_ref, b_ref, c_ref, da_ref, r3_ref, h0_ref, y_ref, h_ref,
                ecs_ref, xw_ref, row_ref, *, reverse):
    q = CHUNK

    @pl.when(pl.program_id(1) == 0)
    def _():
        h_ref[...] = h0_ref[...]

    da = da_ref[0]
    ri = lax.broadcasted_iota(jnp.int32, (q, q), 0)
    ci = lax.broadcasted_iota(jnp.int32, (q, q), 1)
    tri = (ri <= ci) if reverse else (ri >= ci)
    cs = jnp.dot(tri.astype(F32), da, precision=lax.Precision.HIGHEST,
                 preferred_element_type=F32)
    cst = cs.T
    lane_h = lax.broadcasted_iota(jnp.int32, (N_HEADS // 2, LANES), 1) < HALF
    ev = cst[0:N_HEADS // 2]
    od = cst[N_HEADS // 2:N_HEADS]
    row_ref[0] = jnp.where(lane_h, ev, pltpu.roll(od, HALF, 1))
    row_ref[1] = jnp.where(lane_h, pltpu.roll(ev, HALF, 1), od)

    lane_q = lax.broadcasted_iota(jnp.int32, (q, LANES), 1)
    lo_half = lane_q < HALF
    hi = cs.astype(BF16).astype(F32)
    r1 = cs - hi
    mid = r1.astype(BF16).astype(F32)
    lo = (r1 - mid).astype(BF16).astype(F32)
    x1 = jnp.where(lo_half, hi, pltpu.roll(mid, HALF, 1))
    x2 = jnp.where(lo_half, lo, 0.0)
    parts = jnp.concatenate([x1, x2], axis=1).astype(BF16)
    for g in range(N_GROUPS):
        gs = slice(g * GROUP_W, (g + 1) * GROUP_W)
        ecs_ref[:, gs] = jnp.dot(parts, r3_ref[:, gs], preferred_element_type=F32)

    last = 0 if reverse else q - 1
    for g in range(N_GROUPS):
        gs = slice(g * GROUP_W, (g + 1) * GROUP_W)
        w = jnp.exp(ecs_ref[last:last + 1, gs] - ecs_ref[:, gs])
        xw_ref[:, gs] = (xdt_ref[0, :, gs].astype(F32) * w).astype(BF16)

    pos = lax.broadcasted_iota(jnp.int32, (q, LANES), 0)
    s0 = jnp.where(lo_half, lane_q, lane_q - HALF)
    s1 = s0 + HALF
    m0 = (pos <= s0) if reverse else (pos >= s0)
    m1 = (pos <= s1) if reverse else (pos >= s1)
    lo64 = lax.broadcasted_iota(jnp.int32, (HALF, LANES), 1) < HALF
    zero = jnp.zeros((HALF, LANES), BF16)

    for g in range(N_GROUPS):
        gs = slice(g * GROUP_W, (g + 1) * GROUP_W)
        cg = c_ref[0, :, g * D_STATE:(g + 1) * D_STATE]
        bg = b_ref[0, :, g * D_STATE:(g + 1) * D_STATE]
        cb = lax.dot_general(cg, bg, (((1,), (1,)), ((), ())), preferred_element_type=F32)
        cbr = pltpu.roll(cb, HALF, 1)
        cb0 = jnp.where(lo_half, cb, cbr)
        cb1 = jnp.where(lo_half, cbr, cb)
        hg = h_ref[0, :, gs]
        yoff = jnp.dot(cg, hg.astype(BF16), preferred_element_type=F32)
        for p in range(GROUP_W // LANES):
            k = g * (GROUP_W // LANES) + p
            sl = slice(k * LANES, (k + 1) * LANES)
            a = ecs_ref[:, sl]
            t0 = jnp.exp(jnp.where(m0, a - row_ref[0, k:k + 1, :], NEG_BIG)) * cb0
            t1 = jnp.exp(jnp.where(m1, a - row_ref[1, k:k + 1, :], NEG_BIG)) * cb1
            lhs = jnp.concatenate([t0, t1], axis=1).astype(BF16)
            xa = xdt_ref[0, 0:HALF, sl]
            xb = xdt_ref[0, HALF:q, sl]
            rhs = jnp.concatenate([jnp.where(lo64, xa, zero), jnp.where(lo64, zero, xa),
                                   jnp.where(lo64, xb, zero), jnp.where(lo64, zero, xb)], axis=0)
            yd = jnp.dot(lhs, rhs, preferred_element_type=F32)
            y = yd + yoff[:, p * LANES:(p + 1) * LANES] * jnp.exp(a)
            y_ref[0, :, sl] = y.astype(y_ref.dtype)
        st = lax.dot_general(bg, xw_ref[:, gs], (((0,), (0,)), ((), ())), preferred_element_type=F32)
        h_ref[0, :, gs] = hg * jnp.exp(ecs_ref[last:last + 1, gs]) + st


def _ssd(xdt, bm, cm, da, r3, h0, reverse):
    bsz, seq, _ = xdt.shape
    nc = seq // CHUNK
    q = CHUNK
    cidx = (lambda c: nc - 1 - c) if reverse else (lambda c: c)
    dcol = 1 if reverse else 0
    return pl.pallas_call(
        functools.partial(_ssd_kernel, reverse=reverse),
        grid=(bsz, nc),
        in_specs=[
            pl.BlockSpec((1, q, D_INNER), lambda b, c: (b, cidx(c), 0)),
            pl.BlockSpec((1, q, BC_DIM), lambda b, c: (b, cidx(c), 0)),
            pl.BlockSpec((1, q, BC_DIM), lambda b, c: (b, cidx(c), 0)),
            pl.BlockSpec((1, q, LANES), lambda b, c: (b, cidx(c), dcol)),
            pl.BlockSpec((2 * LANES, D_INNER), lambda b, c: (0, 0)),
            pl.BlockSpec((1, D_STATE, D_INNER), lambda b, c: (b, 0, 0)),
        ],
        out_specs=(pl.BlockSpec((1, q, D_INNER), lambda b, c: (b, cidx(c), 0)),
                   pl.BlockSpec((1, D_STATE, D_INNER), lambda b, c: (b, 0, 0))),
        out_shape=(jax.ShapeDtypeStruct((bsz, seq, D_INNER), BF16),
                   jax.ShapeDtypeStruct((bsz, D_STATE, D_INNER), F32)),
        scratch_shapes=[pltpu.VMEM((q, D_INNER), F32),
                        pltpu.VMEM((q, D_INNER), BF16),
                        pltpu.VMEM((2, N_HEADS // 2, LANES), F32)],
        compiler_params=_cparams("arbitrary", "arbitrary"),
        name="ssd_bwd" if reverse else "ssd_fwd",
    )(xdt, bm, cm, da, r3, h0)


def _gate_proja_kernel(yf_ref, yb_ref, xs_ref, sz_ref, dsk_ref, nw_ref, w_ref, o_ref):
    y = (yf_ref[...].astype(F32) + yb_ref[...].astype(F32)
         + dsk_ref[...] * xs_ref[...].astype(F32))
    u = y * sz_ref[...].astype(F32)
    un = _rms(u, nw_ref[...]).astype(BF16)
    o_ref[...] = jnp.dot(un, w_ref[...], preferred_element_type=F32).astype(o_ref.dtype)


def _gate_proja(yf, yb, xs, sz, dsk, nw, w, tm=256):
    t = yf.shape[0]
    tok = pl.BlockSpec((tm, D_INNER), lambda i: (i, 0))
    vec = pl.BlockSpec((1, D_INNER), lambda i: (0, 0))
    return pl.pallas_call(
        _gate_proja_kernel,
        grid=(t // tm,),
        in_specs=[tok, tok, tok, tok, vec, vec,
                  pl.BlockSpec((D_INNER, D_MODEL), lambda i: (0, 0), pipeline_mode=pl.Buffered(1))],
        out_specs=pl.BlockSpec((tm, D_MODEL), lambda i: (i, 0)),
        out_shape=jax.ShapeDtypeStruct((t, D_MODEL), BF16),
        compiler_params=_cparams("arbitrary"),
        name="gate_proj_a",
    )(yf, yb, xs, sz, dsk, nw, w)


CF_TT = 1024
CF_CB = 512


def _cfconv_kernel(gp_ref, gc_ref, gn_ref, w_ref, b_ref, o_ref, win_ref):
    r = pl.program_id(2)
    n = pl.num_programs(2)
    win_ref[0:CF_TT, :] = jnp.where(r > 0, gp_ref[0].astype(F32), 0.0)
    win_ref[CF_TT:2 * CF_TT, :] = gc_ref[0].astype(F32)
    win_ref[2 * CF_TT:3 * CF_TT, :] = jnp.where(r < n - 1, gn_ref[0].astype(F32), 0.0)
    half = CF_KERNEL // 2

    def body(rr, carry):
        base = pl.multiple_of(CF_TT + rr * GRID_W - half * GRID_W, GRID_W)
        acc = jnp.zeros((GRID_W, CF_CB), F32) + b_ref[...]
        for k in range(CF_KERNEL):
            acc = acc + w_ref[k:k + 1, :] * win_ref[pl.ds(base + k * GRID_W, GRID_W), :]
        o_ref[0, pl.ds(pl.multiple_of(rr * GRID_W, GRID_W), GRID_W), :] = acc.astype(o_ref.dtype)
        return carry

    lax.fori_loop(0, CF_TT // GRID_W, body, 0)


def _cfconv(g, w, b):
    bsz, seq, ch = g.shape
    nt = seq // CF_TT
    blk = lambda f: pl.BlockSpec((1, CF_TT, CF_CB), f)
    return pl.pallas_call(
        _cfconv_kernel,
        grid=(bsz, ch // CF_CB, nt),
        in_specs=[blk(lambda bb, c, r: (bb, jnp.maximum(r - 1, 0), c)),
                  blk(lambda bb, c, r: (bb, r, c)),
                  blk(lambda bb, c, r: (bb, jnp.minimum(r + 1, nt - 1), c)),
                  pl.BlockSpec((32, CF_CB), lambda bb, c, r: (0, c)),
                  pl.BlockSpec((1, CF_CB), lambda bb, c, r: (0, c))],
        out_specs=blk(lambda bb, c, r: (bb, r, c)),
        out_shape=jax.ShapeDtypeStruct((bsz, seq, ch), BF16),
        scratch_shapes=[pltpu.VMEM((3 * CF_TT, CF_CB), F32)],
        compiler_params=_cparams("arbitrary", "arbitrary", "arbitrary"),
        name="conformer_conv",
    )(g, g, g, w, b)


def _merge_kernel(cf_ref, ya_ref, sg_ref, x_ref, lng_ref, lnb_ref, wpb_ref, wout_ref, g1_ref, o_ref):
    cf = cf_ref[...].astype(F32)
    mu = jnp.mean(cf, axis=-1, keepdims=True)
    xc = cf - mu
    var = jnp.mean(xc * xc, axis=-1, keepdims=True)
    ln = xc * lax.rsqrt(var + EPS) * lng_ref[...] + lnb_ref[...]
    yb = jnp.dot(_silu(ln).astype(BF16), wpb_ref[...], preferred_element_type=F32)
    m = (sg_ref[:, 0:D_MODEL].astype(F32) * ya_ref[...].astype(F32)
         + sg_ref[:, D_MODEL:2 * D_MODEL].astype(F32) * yb)
    mix = jnp.dot(m.astype(BF16), wout_ref[...], preferred_element_type=F32)
    o_ref[...] = x_ref[...] + g1_ref[0] * mix


def _merge(cf, ya, sg, x2d, lng, lnb, wpb, wout, g1, tokens_per_batch, tm=256):
    t = x2d.shape[0]
    tpb = tokens_per_batch // tm
    tok = lambda w: pl.BlockSpec((tm, w), lambda i: (i, 0))
    vec = pl.BlockSpec((1, D_MODEL), lambda i: (0, 0))
    wspec = pl.BlockSpec((D_MODEL, D_MODEL), lambda i: (0, 0), pipeline_mode=pl.Buffered(1))
    return pl.pallas_call(
        _merge_kernel,
        grid=(t // tm,),
        in_specs=[tok(D_MODEL), tok(D_MODEL), tok(2 * D_MODEL), tok(D_MODEL), vec, vec, wspec, wspec,
                  pl.BlockSpec((1, 1, D_MODEL), lambda i: (i // tpb, 0, 0))],
        out_specs=tok(D_MODEL),
        out_shape=jax.ShapeDtypeStruct((t, D_MODEL), F32),
        compiler_params=_cparams("arbitrary"),
        name="merge_out_proj",
    )(cf, ya, sg, x2d, lng, lnb, wpb, wout, g1)


def _ffn_kernel(x_ref, nw_ref, sc_ref, sh_ref, g2_ref, wg_ref, wu_ref, wd_ref, nf_ref, o_ref,
                hx_ref, acc_ref):
    f = pl.program_id(1)

    @pl.when(f == 0)
    def _():
        y = _rms(x_ref[...], nw_ref[...])
        hx_ref[...] = (y * (1.0 + sc_ref[0]) + sh_ref[0]).astype(BF16)
        acc_ref[...] = jnp.zeros_like(acc_ref)

    h = hx_ref[...]
    gate = jnp.dot(h, wg_ref[...], preferred_element_type=F32)
    up = jnp.dot(h, wu_ref[...], preferred_element_type=F32)
    act = (_silu(gate) * up).astype(BF16)
    acc_ref[...] += jnp.dot(act, wd_ref[...], preferred_element_type=F32)

    @pl.when(f == pl.num_programs(1) - 1)
    def _():
        x2 = x_ref[...] + g2_ref[0] * acc_ref[...]
        o_ref[...] = _rms(x2, nf_ref[...])


def _ffn(x1, nw, sc2, sh2, g2, wg, wu, wd, nf, tokens_per_batch, tm=512, tf=512):
    t = x1.shape[0]
    tpb = tokens_per_batch // tm
    vec = pl.BlockSpec((1, D_MODEL), lambda i, f: (0, 0))
    mod = pl.BlockSpec((1, 1, D_MODEL), lambda i, f: (i // tpb, 0, 0))
    return pl.pallas_call(
        _ffn_kernel,
        grid=(t // tm, D_FF // tf),
        in_specs=[pl.BlockSpec((tm, D_MODEL), lambda i, f: (i, 0)), vec, mod, mod, mod,
                  pl.BlockSpec((D_MODEL, tf), lambda i, f: (0, f)),
                  pl.BlockSpec((D_MODEL, tf), lambda i, f: (0, f)),
                  pl.BlockSpec((tf, D_MODEL), lambda i, f: (f, 0)), vec],
        out_specs=pl.BlockSpec((tm, D_MODEL), lambda i, f: (i, 0)),
        out_shape=jax.ShapeDtypeStruct((t, D_MODEL), F32),
        scratch_shapes=[pltpu.VMEM((tm, D_MODEL), BF16), pltpu.VMEM((tm, D_MODEL), F32)],
        compiler_params=_cparams("arbitrary", "arbitrary"),
        name="swiglu_ffn",
    )(x1, nw, sc2, sh2, g2, wg, wu, wd, nf)


_HEAD_PERM = np.concatenate([np.arange(0, N_HEADS, 2), np.arange(1, N_HEADS, 2)])


def _expand_matrix(row_blocks):
    col_head = np.arange(D_INNER) // HEAD_DIM
    pat = (_HEAD_PERM[:, None] == col_head[None, :]).astype(np.float32)
    zero = np.zeros_like(pat)
    return jnp.asarray(np.concatenate([pat if on else zero for on in row_blocks], axis=0), BF16)


def _front(h2d, bsz, seq, wxbc, wdt4, cw, cb, dtb4, alog4, rf, rb, r3, h0f, h0b):
    xbc = _mm(h2d, wxbc, "none", BF16).reshape(bsz, seq, CONV_DIM)
    dt4 = _mm(h2d, wdt4, "none", F32).reshape(bsz, seq, 2 * LANES)
    xs, xdf, xdb, bm, cm, da = _conv(xbc, dt4, cw, cb, dtb4, alog4, rf, rb)
    yf, hf = _ssd(xdf, bm, cm, da, r3, h0f, reverse=False)
    yb, hb = _ssd(xdb, bm, cm, da, r3, h0b, reverse=True)
    return xs, yf, yb, hf, hb


@jax.jit
def kernel(x, c, ctx, c_ctx, w_mod, b_mod, norm_mix, w_in, ssm_conv_w, ssm_conv_b, dt_bias, a_log,
           d_skip, ssm_norm, cf_conv_w, cf_conv_b, cf_ln_g, cf_ln_b, w_proj_a, w_proj_b, w_out,
           norm_ffn, w_ffn_gate, w_ffn_up, w_ffn_down, norm_final):
    bsz, seq, _ = x.shape
    ctx_len = ctx.shape[1]
    t = bsz * seq

    cc = jnp.zeros((8, D_MODEL), F32).at[0:bsz].set(c).at[bsz].set(c_ctx)
    mod = _ada(cc, w_mod[0], b_mod[0])
    sh1, sc1, g1, sh2, sc2, g2 = [m.reshape(8, 1, D_MODEL) for m in jnp.split(mod, 6, axis=-1)]
    lat = lambda m: m[0:bsz]
    cx = lambda m: jnp.broadcast_to(m[bsz:bsz + 1], (bsz, 1, D_MODEL))

    w = w_in[0]
    wz = w[:, 0:OFF_Z].astype(BF16)
    wxbc = w[:, OFF_Z:OFF_XBC].astype(BF16)
    wdt = w[:, OFF_XBC:OFF_DT]
    wdt_f = wdt[:, 0:N_HEADS][:, _HEAD_PERM]
    wdt_b = wdt[:, N_HEADS:2 * N_HEADS][:, _HEAD_PERM]
    wdt4 = jnp.concatenate([wdt_f, wdt_b, wdt_b, wdt_f], axis=1).astype(BF16)
    wglu_u = w[:, OFF_DT:OFF_DT + D_MODEL].astype(BF16)
    wglu_v = w[:, OFF_DT + D_MODEL:OFF_GLU].astype(BF16)
    wgates = w[:, OFF_GLU:].astype(BF16)
    perm2 = lambda p: jnp.concatenate([p[0][_HEAD_PERM], p[1][_HEAD_PERM],
                                       p[1][_HEAD_PERM], p[0][_HEAD_PERM]]).reshape(1, 2 * LANES)
    dtb4 = perm2(dt_bias[0])
    alog4 = perm2(a_log[0])
    cw = jnp.zeros((8, CONV_DIM), F32).at[0:SSM_CONV].set(ssm_conv_w[0])
    cb = ssm_conv_b[0].reshape(1, CONV_DIM)
    rf = _expand_matrix([True, False, True, False])
    rb = _expand_matrix([False, True, False, True])
    r3 = _expand_matrix([True, True, True, False])
    dsk = jnp.repeat(d_skip[0], HEAD_DIM).reshape(1, D_INNER)
    cfw = jnp.zeros((32, D_MODEL), F32).at[0:CF_KERNEL].set(cf_conv_w[0])

    hc = _normmod(ctx.reshape(bsz * ctx_len, D_MODEL), norm_mix[0], cx(sc1), cx(sh1), ctx_len, tm=256)
    zero_state = jnp.zeros((bsz, D_STATE, D_INNER), F32)
    _, _, _, hf, hb = _front(hc, bsz, ctx_len, wxbc, wdt4, cw, cb, dtb4, alog4, rf, rb, r3,
                             zero_state, zero_state)

    x2d = x.reshape(t, D_MODEL)
    hx = _normmod(x2d, norm_mix[0], lat(sc1), lat(sh1), seq)
    xs, yf, yb, _, _ = _front(hx, bsz, seq, wxbc, wdt4, cw, cb, dtb4, alog4, rf, rb, r3, hf, hb)
    sz = _mm(hx, wz, "silu", BF16)
    glu = _mm_glu(hx, wglu_u, wglu_v)
    sg = _mm(hx, wgates, "sigmoid", BF16)

    ya = _gate_proja(yf.reshape(t, D_INNER), yb.reshape(t, D_INNER), xs.reshape(t, D_INNER), sz,
                     dsk, ssm_norm[0].reshape(1, D_INNER), w_proj_a[0].astype(BF16))
    cf = _cfconv(glu.reshape(bsz, seq, D_MODEL), cfw, cf_conv_b[0].reshape(1, D_MODEL))
    x1 = _merge(cf.reshape(t, D_MODEL), ya, sg, x2d, cf_ln_g[0].reshape(1, D_MODEL),
                cf_ln_b[0].reshape(1, D_MODEL), w_proj_b[0].astype(BF16), w_out[0].astype(BF16),
                lat(g1), seq)
    out = _ffn(x1, norm_ffn[0].reshape(1, D_MODEL), lat(sc2), lat(sh2), lat(g2),
               w_ffn_gate[0].astype(BF16), w_ffn_up[0].astype(BF16), w_ffn_down[0].astype(BF16),
               norm_final.reshape(1, D_MODEL), seq)
    return out.reshape(bsz, seq, D_MODEL)
```

```python
import functools

import jax
import jax.numpy as jnp
import numpy as np
from jax import lax
from jax.experimental import pallas as pl
from jax.experimental.pallas import tpu as pltpu

F32 = jnp.float32
BF16 = jnp.bfloat16

D_MODEL = 2048
GRID_W = 64
D_INNER = 2 * D_MODEL
HEAD_DIM = 64
N_HEADS = D_INNER // HEAD_DIM
N_GROUPS = 8
GROUP_W = D_INNER // N_GROUPS
D_STATE = 128
SSM_CONV = 5
CHUNK = 128
BC_DIM = N_GROUPS * D_STATE
CONV_DIM = D_INNER + 2 * BC_DIM
CF_KERNEL = 31
D_FF = 5632
EPS = 1e-6
OFF_Z = D_INNER
OFF_XBC = OFF_Z + CONV_DIM
OFF_DT = OFF_XBC + 2 * N_HEADS
OFF_GLU = OFF_DT + 2 * D_MODEL

LANES = 128
HALF = LANES // 2
MXU_N = 256
VMEM_LIMIT = 56 * 1024 * 1024
NEG_BIG = -1e30
HALO = 16


def _cparams(*sem):
    return pltpu.CompilerParams(dimension_semantics=sem, vmem_limit_bytes=VMEM_LIMIT)


def _sigmoid(v):
    return jax.nn.sigmoid(v)


def _silu(v):
    return v * jax.nn.sigmoid(v)


def _rms(v, w):
    return v * lax.rsqrt(jnp.mean(v * v, axis=-1, keepdims=True) + EPS) * w


def _ada_kernel(c_ref, w_ref, b_ref, o_ref):
    s = _silu(c_ref[...]).astype(BF16)
    o_ref[...] = jnp.dot(s, w_ref[...].astype(BF16), preferred_element_type=F32) + b_ref[...]


def _ada(cc, w_mod, b_mod):
    n = w_mod.shape[1]
    tn = 1024
    return pl.pallas_call(
        _ada_kernel,
        grid=(n // tn,),
        in_specs=[pl.BlockSpec((8, D_MODEL), lambda j: (0, 0)),
                  pl.BlockSpec((D_MODEL, tn), lambda j: (0, j)),
                  pl.BlockSpec((1, tn), lambda j: (0, j))],
        out_specs=pl.BlockSpec((8, tn), lambda j: (0, j)),
        out_shape=jax.ShapeDtypeStruct((8, n), F32),
        compiler_params=_cparams("arbitrary"),
        name="ada_params",
    )(cc, w_mod, b_mod.reshape(1, n))


def _normmod_kernel(xp_ref, x_ref, xn_ref, w_ref, sc_ref, sh_ref, o_ref):
    i = pl.program_id(1)
    n = pl.num_programs(1)
    tm = x_ref.shape[1]

    def nm(v):
        return _rms(v, w_ref[...]) * (1.0 + sc_ref[0]) + sh_ref[0]

    o_ref[0, 0:tm, :] = nm(x_ref[0]).astype(o_ref.dtype)
    nxt = jnp.where(i < n - 1, nm(xn_ref[0]), 0.0)
    prv = jnp.where(i > 0, nm(xp_ref[0]), 0.0)
    o_ref[0, tm:tm + HALO, :] = jnp.concatenate([nxt, prv], axis=0).astype(o_ref.dtype)


def _normmod(x3d, w, sc, sh, tm):
    bsz, seq, _ = x3d.shape
    nt = seq // tm
    r8 = tm // 8
    n8 = seq // 8
    mod = pl.BlockSpec((1, 1, D_MODEL), lambda b, i: (b, 0, 0))
    return pl.pallas_call(
        _normmod_kernel,
        grid=(bsz, nt),
        in_specs=[pl.BlockSpec((1, 8, D_MODEL), lambda b, i: (b, jnp.maximum(i * r8 - 1, 0), 0)),
                  pl.BlockSpec((1, tm, D_MODEL), lambda b, i: (b, i, 0)),
                  pl.BlockSpec((1, 8, D_MODEL), lambda b, i: (b, jnp.minimum((i + 1) * r8, n8 - 1), 0)),
                  pl.BlockSpec((1, D_MODEL), lambda b, i: (0, 0)), mod, mod],
        out_specs=pl.BlockSpec((1, tm + HALO, D_MODEL), lambda b, i: (b * nt + i, 0, 0)),
        out_shape=jax.ShapeDtypeStruct((bsz * nt, tm + HALO, D_MODEL), BF16),
        compiler_params=_cparams("arbitrary", "arbitrary"),
        name="norm_modulate",
    )(x3d, x3d, x3d, w.reshape(1, D_MODEL), sc, sh)


def _mm_kernel(a_ref, w_ref, o_ref, *, act):
    acc = jnp.dot(a_ref[0], w_ref[...], preferred_element_type=F32)
    if act == "silu":
        acc = _silu(acc)
    elif act == "sigmoid":
        acc = _sigmoid(acc)
    o_ref[...] = acc.astype(o_ref.dtype)


def _mm(a3, w, act, out_dtype, tn=1024):
    ntile, tme, k = a3.shape
    tm = tme - HALO
    n = w.shape[1]
    tn = min(tn, n)
    return pl.pallas_call(
        functools.partial(_mm_kernel, act=act),
        grid=(n // tn, ntile),
        in_specs=[pl.BlockSpec((1, tm, k), lambda j, i: (i, 0, 0)),
                  pl.BlockSpec((k, tn), lambda j, i: (0, j))],
        out_specs=pl.BlockSpec((tm, tn), lambda j, i: (i, j)),
        out_shape=jax.ShapeDtypeStruct((ntile * tm, n), out_dtype),
        compiler_params=_cparams("arbitrary", "arbitrary"),
        name="proj_" + act,
    )(a3, w)


def _glu_kernel(a_ref, wu_ref, wv_ref, o_ref):
    a = a_ref[0]
    u = jnp.dot(a, wu_ref[...], preferred_element_type=F32)
    v = jnp.dot(a, wv_ref[...], preferred_element_type=F32)
    o_ref[...] = (u * _sigmoid(v)).astype(o_ref.dtype)


def _mm_glu(a3, wu, wv, tn=512):
    ntile, tme, k = a3.shape
    tm = tme - HALO
    n = wu.shape[1]
    return pl.pallas_call(
        _glu_kernel,
        grid=(n // tn, ntile),
        in_specs=[pl.BlockSpec((1, tm, k), lambda j, i: (i, 0, 0)),
                  pl.BlockSpec((k, tn), lambda j, i: (0, j)),
                  pl.BlockSpec((k, tn), lambda j, i: (0, j))],
        out_specs=pl.BlockSpec((tm, tn), lambda j, i: (i, j)),
        out_shape=jax.ShapeDtypeStruct((ntile * tm, n), BF16),
        compiler_params=_cparams("arbitrary", "arbitrary"),
        name="proj_glu",
    )(a3, wu, wv)


def _dt_kernel(a_ref, w_ref, bias_ref, alog_ref, dt_ref, da_ref):
    raw = jnp.dot(a_ref[0], w_ref[...], preferred_element_type=F32) + bias_ref[...]
    dt = jnp.maximum(raw, 0.0) + jnp.log1p(jnp.exp(-jnp.abs(raw)))
    dt_ref[...] = dt
    da_ref[...] = dt * (-jnp.exp(alog_ref[...]))


def _mm_dt(a3, w, bias, alog):
    ntile, tme, k = a3.shape
    tm = tme - HALO
    n = w.shape[1]
    vec = pl.BlockSpec((1, n), lambda i: (0, 0))
    out = pl.BlockSpec((tm, n), lambda i: (i, 0))
    shape = jax.ShapeDtypeStruct((ntile * tm, n), F32)
    return pl.pallas_call(
        _dt_kernel,
        grid=(ntile,),
        in_specs=[pl.BlockSpec((1, tm, k), lambda i: (i, 0, 0)),
                  pl.BlockSpec((k, n), lambda i: (0, 0)), vec, vec],
        out_specs=(out, out),
        out_shape=(shape, shape),
        compiler_params=_cparams("arbitrary"),
        name="proj_dt",
    )(a3, w, bias, alog)


def _xbc_kernel(a_ref, w_ref, cw_ref, cb_ref, o_ref):
    a = a_ref[0]
    n = a.shape[0]
    tm = n - HALO
    for nb in range(o_ref.shape[1] // MXU_N):
        cs = slice(nb * MXU_N, (nb + 1) * MXU_N)
        p = jnp.dot(a, w_ref[:, cs], preferred_element_type=F32)
        acc = cb_ref[:, cs] + cw_ref[2:3, cs] * p
        for k in (0, 1, 3, 4):
            acc = acc + cw_ref[k:k + 1, cs] * pltpu.roll(p, (2 - k) % n, 0)
        o_ref[:, cs] = _silu(acc[0:tm]).astype(o_ref.dtype)


def _mm_xbc(a3, w, cw, cb, tn=1024):
    ntile, tme, k = a3.shape
    tm = tme - HALO
    n = w.shape[1]
    return pl.pallas_call(
        _xbc_kernel,
        grid=(n // tn, ntile),
        in_specs=[pl.BlockSpec((1, tme, k), lambda j, i: (i, 0, 0)),
                  pl.BlockSpec((k, tn), lambda j, i: (0, j)),
                  pl.BlockSpec((8, tn), lambda j, i: (0, j)),
                  pl.BlockSpec((1, tn), lambda j, i: (0, j))],
        out_specs=pl.BlockSpec((tm, tn), lambda j, i: (i, j)),
        out_shape=jax.ShapeDtypeStruct((ntile * tm, n), BF16),
        compiler_params=_cparams("arbitrary", "arbitrary"),
        name="proj_xbc_conv",
    )(a3, w, cw, cb)


def _ssd_kernel(xs_ref, b_ref, c_ref, dt_ref, da_ref, r3_ref, r2_ref, h0_ref, y_ref, h_ref,
                ecs_ref, xdt_ref, xw_ref, row_ref, *, reverse):
    q = CHUNK

    @pl.when(pl.program_id(1) == 0)
    def _():
        h_ref[...] = h0_ref[...]

    da = da_ref[0]
    ri = lax.broadcasted_iota(jnp.int32, (q, q), 0)
    ci = lax.broadcasted_iota(jnp.int32, (q, q), 1)
    tri = (ri <= ci) if reverse else (ri >= ci)
    cs = jnp.dot(tri.astype(F32), da, precision=lax.Precision.HIGHEST,
                 preferred_element_type=F32)
    cst = cs.T
    lane_h = lax.broadcasted_iota(jnp.int32, (N_HEADS // 2, LANES), 1) < HALF
    ev = cst[0:N_HEADS // 2]
    od = cst[N_HEADS // 2:N_HEADS]
    row_ref[0] = jnp.where(lane_h, ev, pltpu.roll(od, HALF, 1))
    row_ref[1] = jnp.where(lane_h, pltpu.roll(ev, HALF, 1), od)

    lane_q = lax.broadcasted_iota(jnp.int32, (q, LANES), 1)
    lo_half = lane_q < HALF
    hi = cs.astype(BF16).astype(F32)
    r1 = cs - hi
    mid = r1.astype(BF16).astype(F32)
    lo = (r1 - mid).astype(BF16).astype(F32)
    x1 = jnp.where(lo_half, hi, pltpu.roll(mid, HALF, 1))
    x2 = jnp.where(lo_half, lo, 0.0)
    cparts = jnp.concatenate([x1, x2], axis=1).astype(BF16)
    dt = dt_ref[0]
    dhi = dt.astype(BF16)
    dmid = (dt - dhi.astype(F32)).astype(BF16)
    dparts = jnp.concatenate([dhi, dmid], axis=1)

    last = 0 if reverse else q - 1
    for g in range(N_GROUPS):
        gs = slice(g * GROUP_W, (g + 1) * GROUP_W)
        ecs = jnp.dot(cparts, r3_ref[:, gs], preferred_element_type=F32)
        ecs_ref[:, gs] = ecs
        edt = jnp.dot(dparts, r2_ref[:, gs], preferred_element_type=F32)
        xdt = xs_ref[0, :, gs].astype(F32) * edt
        xdt_ref[:, gs] = xdt.astype(BF16)
        xw_ref[:, gs] = (xdt * jnp.exp(ecs[last:last + 1, :] - ecs)).astype(BF16)

    pos = lax.broadcasted_iota(jnp.int32, (q, LANES), 0)
    s0 = jnp.where(lo_half, lane_q, lane_q - HALF)
    s1 = s0 + HALF
    m0 = (pos <= s0) if reverse else (pos >= s0)
    m1 = (pos <= s1) if reverse else (pos >= s1)
    lo64 = lax.broadcasted_iota(jnp.int32, (HALF, LANES), 1) < HALF
    zero = jnp.zeros((HALF, LANES), BF16)

    for g in range(N_GROUPS):
        gs = slice(g * GROUP_W, (g + 1) * GROUP_W)
        cg = c_ref[0, :, g * D_STATE:(g + 1) * D_STATE]
        bg = b_ref[0, :, g * D_STATE:(g + 1) * D_STATE]
        cb = lax.dot_general(cg, bg, (((1,), (1,)), ((), ())), preferred_element_type=F32)
        cbr = pltpu.roll(cb, HALF, 1)
        cb0 = jnp.where(lo_half, cb, cbr)
        cb1 = jnp.where(lo_half, cbr, cb)
        hg = h_ref[0, :, gs]
        yoff = jnp.dot(cg, hg.astype(BF16), preferred_element_type=F32)
        for p in range(GROUP_W // LANES):
            k = g * (GROUP_W // LANES) + p
            sl = slice(k * LANES, (k + 1) * LANES)
            a = ecs_ref[:, sl]
            t0 = jnp.exp(jnp.where(m0, a - row_ref[0, k:k + 1, :], NEG_BIG)) * cb0
            t1 = jnp.exp(jnp.where(m1, a - row_ref[1, k:k + 1, :], NEG_BIG)) * cb1
            lhs = jnp.concatenate([t0, t1], axis=1).astype(BF16)
            xa = xdt_ref[0:HALF, sl]
            xb = xdt_ref[HALF:q, sl]
            rhs = jnp.concatenate([jnp.where(lo64, xa, zero), jnp.where(lo64, zero, xa),
                                   jnp.where(lo64, xb, zero), jnp.where(lo64, zero, xb)], axis=0)
            yd = jnp.dot(lhs, rhs, preferred_element_type=F32)
            y = yd + yoff[:, p * LANES:(p + 1) * LANES] * jnp.exp(a)
            y_ref[0, :, sl] = y.astype(y_ref.dtype)
        st = lax.dot_general(bg, xw_ref[:, gs], (((0,), (0,)), ((), ())), preferred_element_type=F32)
        h_ref[0, :, gs] = hg * jnp.exp(ecs_ref[last:last + 1, gs]) + st


def _ssd(xbc, dt, da, r3, r2, h0, reverse):
    bsz, seq, _ = xbc.shape
    nc = seq // CHUNK
    q = CHUNK
    cidx = (lambda c: nc - 1 - c) if reverse else (lambda c: c)
    dcol = 1 if reverse else 0
    bcol = D_INNER // BC_DIM
    const = lambda r, c: pl.BlockSpec((r, c), lambda b, cc: (0, 0), pipeline_mode=pl.Buffered(1))
    return pl.pallas_call(
        functools.partial(_ssd_kernel, reverse=reverse),
        grid=(bsz, nc),
        in_specs=[
            pl.BlockSpec((1, q, D_INNER), lambda b, c: (b, cidx(c), 0)),
            pl.BlockSpec((1, q, BC_DIM), lambda b, c: (b, cidx(c), bcol)),
            pl.BlockSpec((1, q, BC_DIM), lambda b, c: (b, cidx(c), bcol + 1)),
            pl.BlockSpec((1, q, LANES), lambda b, c: (b, cidx(c), dcol)),
            pl.BlockSpec((1, q, LANES), lambda b, c: (b, cidx(c), dcol)),
            const(2 * LANES, D_INNER), const(2 * LANES, D_INNER),
            pl.BlockSpec((1, D_STATE, D_INNER), lambda b, c: (b, 0, 0)),
        ],
        out_specs=(pl.BlockSpec((1, q, D_INNER), lambda b, c: (b, cidx(c), 0)),
                   pl.BlockSpec((1, D_STATE, D_INNER), lambda b, c: (b, 0, 0))),
        out_shape=(jax.ShapeDtypeStruct((bsz, seq, D_INNER), BF16),
                   jax.ShapeDtypeStruct((bsz, D_STATE, D_INNER), F32)),
        scratch_shapes=[pltpu.VMEM((q, D_INNER), F32),
                        pltpu.VMEM((q, D_INNER), BF16),
                        pltpu.VMEM((q, D_INNER), BF16),
                        pltpu.VMEM((2, N_HEADS // 2, LANES), F32)],
        compiler_params=_cparams("arbitrary", "arbitrary"),
        name="ssd_bwd" if reverse else "ssd_fwd",
    )(xbc, xbc, xbc, dt, da, r3, r2, h0)


def _gate_proja_kernel(yf_ref, yb_ref, xs_ref, sz_ref, dsk_ref, nw_ref, w_ref, o_ref):
    y = (yf_ref[...].astype(F32) + yb_ref[...].astype(F32)
         + dsk_ref[...] * xs_ref[...].astype(F32))
    u = y * sz_ref[...].astype(F32)
    un = _rms(u, nw_ref[...]).astype(BF16)
    o_ref[...] = jnp.dot(un, w_ref[...], preferred_element_type=F32).astype(o_ref.dtype)


def _gate_proja(yf, yb, xbc, sz, dsk, nw, w, tm=256):
    t = yf.shape[0]
    tok = pl.BlockSpec((tm, D_INNER), lambda i: (i, 0))
    vec = pl.BlockSpec((1, D_INNER), lambda i: (0, 0))
    return pl.pallas_call(
        _gate_proja_kernel,
        grid=(t // tm,),
        in_specs=[tok, tok, tok, tok, vec, vec,
                  pl.BlockSpec((D_INNER, D_MODEL), lambda i: (0, 0), pipeline_mode=pl.Buffered(1))],
        out_specs=pl.BlockSpec((tm, D_MODEL), lambda i: (i, 0)),
        out_shape=jax.ShapeDtypeStruct((t, D_MODEL), BF16),
        compiler_params=_cparams("arbitrary"),
        name="gate_proj_a",
    )(yf, yb, xbc, sz, dsk, nw, w)


CF_TT = 1024
CF_CB = 512


def _cfconv_kernel(gp_ref, gc_ref, gn_ref, w_ref, b_ref, o_ref, win_ref):
    r = pl.program_id(2)
    n = pl.num_programs(2)
    win_ref[0:CF_TT, :] = jnp.where(r > 0, gp_ref[0].astype(F32), 0.0)
    win_ref[CF_TT:2 * CF_TT, :] = gc_ref[0].astype(F32)
    win_ref[2 * CF_TT:3 * CF_TT, :] = jnp.where(r < n - 1, gn_ref[0].astype(F32), 0.0)
    half = CF_KERNEL // 2

    def body(rr, carry):
        base = pl.multiple_of(CF_TT + rr * GRID_W - half * GRID_W, GRID_W)
        acc = jnp.zeros((GRID_W, CF_CB), F32) + b_ref[...]
        for k in range(CF_KERNEL):
            acc = acc + w_ref[k:k + 1, :] * win_ref[pl.ds(base + k * GRID_W, GRID_W), :]
        o_ref[0, pl.ds(pl.multiple_of(rr * GRID_W, GRID_W), GRID_W), :] = acc.astype(o_ref.dtype)
        return carry

    lax.fori_loop(0, CF_TT // GRID_W, body, 0)


def _cfconv(g, w, b):
    bsz, seq, ch = g.shape
    nt = seq // CF_TT
    blk = lambda f: pl.BlockSpec((1, CF_TT, CF_CB), f)
    return pl.pallas_call(
        _cfconv_kernel,
        grid=(bsz, ch // CF_CB, nt),
        in_specs=[blk(lambda bb, c, r: (bb, jnp.maximum(r - 1, 0), c)),
                  blk(lambda bb, c, r: (bb, r, c)),
                  blk(lambda bb, c, r: (bb, jnp.minimum(r + 1, nt - 1), c)),
                  pl.BlockSpec((32, CF_CB), lambda bb, c, r: (0, c)),
                  pl.BlockSpec((1, CF_CB), lambda bb, c, r: (0, c))],
        out_specs=blk(lambda bb, c, r: (bb, r, c)),
        out_shape=jax.ShapeDtypeStruct((bsz, seq, ch), BF16),
        scratch_shapes=[pltpu.VMEM((3 * CF_TT, CF_CB), F32)],
        compiler_params=_cparams("arbitrary", "arbitrary", "arbitrary"),
        name="conformer_conv",
    )(g, g, g, w, b)


def _merge_kernel(cf_ref, ya_ref, sg_ref, x_ref, lng_ref, lnb_ref, wpb_ref, wout_ref, g1_ref, o_ref):
    cf = cf_ref[...].astype(F32)
    mu = jnp.mean(cf, axis=-1, keepdims=True)
    xc = cf - mu
    var = jnp.mean(xc * xc, axis=-1, keepdims=True)
    ln = xc * lax.rsqrt(var + EPS) * lng_ref[...] + lnb_ref[...]
    yb = jnp.dot(_silu(ln).astype(BF16), wpb_ref[...], preferred_element_type=F32)
    m = (sg_ref[:, 0:D_MODEL].astype(F32) * ya_ref[...].astype(F32)
         + sg_ref[:, D_MODEL:2 * D_MODEL].astype(F32) * yb)
    mix = jnp.dot(m.astype(BF16), wout_ref[...], preferred_element_type=F32)
    o_ref[...] = x_ref[...] + g1_ref[0] * mix


def _merge(cf, ya, sg, x2d, lng, lnb, wpb, wout, g1, tokens_per_batch, tm=256):
    t = x2d.shape[0]
    tpb = tokens_per_batch // tm
    tok = lambda w: pl.BlockSpec((tm, w), lambda i: (i, 0))
    vec = pl.BlockSpec((1, D_MODEL), lambda i: (0, 0))
    wspec = pl.BlockSpec((D_MODEL, D_MODEL), lambda i: (0, 0), pipeline_mode=pl.Buffered(1))
    return pl.pallas_call(
        _merge_kernel,
        grid=(t // tm,),
        in_specs=[tok(D_MODEL), tok(D_MODEL), tok(2 * D_MODEL), tok(D_MODEL), vec, vec, wspec, wspec,
                  pl.BlockSpec((1, 1, D_MODEL), lambda i: (i // tpb, 0, 0))],
        out_specs=tok(D_MODEL),
        out_shape=jax.ShapeDtypeStruct((t, D_MODEL), F32),
        compiler_params=_cparams("arbitrary"),
        name="merge_out_proj",
    )(cf, ya, sg, x2d, lng, lnb, wpb, wout, g1)


def _ffn_kernel(x_ref, nw_ref, sc_ref, sh_ref, g2_ref, wg_ref, wu_ref, wd_ref, nf_ref, o_ref,
                hx_ref, acc_ref):
    f = pl.program_id(1)

    @pl.when(f == 0)
    def _():
        y = _rms(x_ref[...], nw_ref[...])
        hx_ref[...] = (y * (1.0 + sc_ref[0]) + sh_ref[0]).astype(BF16)
        acc_ref[...] = jnp.zeros_like(acc_ref)

    h = hx_ref[...]
    gate = jnp.dot(h, wg_ref[...], preferred_element_type=F32)
    up = jnp.dot(h, wu_ref[...], preferred_element_type=F32)
    act = (_silu(gate) * up).astype(BF16)
    acc_ref[...] += jnp.dot(act, wd_ref[...], preferred_element_type=F32)

    @pl.when(f == pl.num_programs(1) - 1)
    def _():
        x2 = x_ref[...] + g2_ref[0] * acc_ref[...]
        o_ref[...] = _rms(x2, nf_ref[...])


def _ffn(x1, nw, sc2, sh2, g2, wg, wu, wd, nf, tokens_per_batch, tm=512, tf=512):
    t = x1.shape[0]
    tpb = tokens_per_batch // tm
    vec = pl.BlockSpec((1, D_MODEL), lambda i, f: (0, 0))
    mod = pl.BlockSpec((1, 1, D_MODEL), lambda i, f: (i // tpb, 0, 0))
    return pl.pallas_call(
        _ffn_kernel,
        grid=(t // tm, D_FF // tf),
        in_specs=[pl.BlockSpec((tm, D_MODEL), lambda i, f: (i, 0)), vec, mod, mod, mod,
                  pl.BlockSpec((D_MODEL, tf), lambda i, f: (0, f)),
                  pl.BlockSpec((D_MODEL, tf), lambda i, f: (0, f)),
                  pl.BlockSpec((tf, D_MODEL), lambda i, f: (f, 0)), vec],
        out_specs=pl.BlockSpec((tm, D_MODEL), lambda i, f: (i, 0)),
        out_shape=jax.ShapeDtypeStruct((t, D_MODEL), F32),
        scratch_shapes=[pltpu.VMEM((tm, D_MODEL), BF16), pltpu.VMEM((tm, D_MODEL), F32)],
        compiler_params=_cparams("arbitrary", "arbitrary"),
        name="swiglu_ffn",
    )(x1, nw, sc2, sh2, g2, wg, wu, wd, nf)


_HEAD_PERM = np.concatenate([np.arange(0, N_HEADS, 2), np.arange(1, N_HEADS, 2)])


def _expand_matrix(row_blocks):
    col_head = np.arange(D_INNER) // HEAD_DIM
    pat = (_HEAD_PERM[:, None] == col_head[None, :]).astype(np.float32)
    zero = np.zeros_like(pat)
    return jnp.asarray(np.concatenate([pat if on else zero for on in row_blocks], axis=0), BF16)


def _front(h3, bsz, seq, wxbc, wdt4, cw, cb, dtb4, alog4, r3, r2, h0f, h0b):
    xbc = _mm_xbc(h3, wxbc, cw, cb).reshape(bsz, seq, CONV_DIM)
    dt, da = _mm_dt(h3, wdt4, dtb4, alog4)
    dt = dt.reshape(bsz, seq, 2 * LANES)
    da = da.reshape(bsz, seq, 2 * LANES)
    yf, hf = _ssd(xbc, dt, da, r3, r2, h0f, reverse=False)
    yb, hb = _ssd(xbc, dt, da, r3, r2, h0b, reverse=True)
    return xbc, yf, yb, hf, hb


@jax.jit
def kernel(x, c, ctx, c_ctx, w_mod, b_mod, norm_mix, w_in, ssm_conv_w, ssm_conv_b, dt_bias, a_log,
           d_skip, ssm_norm, cf_conv_w, cf_conv_b, cf_ln_g, cf_ln_b, w_proj_a, w_proj_b, w_out,
           norm_ffn, w_ffn_gate, w_ffn_up, w_ffn_down, norm_final):
    bsz, seq, _ = x.shape
    ctx_len = ctx.shape[1]
    t = bsz * seq

    cc = jnp.zeros((8, D_MODEL), F32).at[0:bsz].set(c).at[bsz].set(c_ctx)
    mod = _ada(cc, w_mod[0], b_mod[0])
    sh1, sc1, g1, sh2, sc2, g2 = [m.reshape(8, 1, D_MODEL) for m in jnp.split(mod, 6, axis=-1)]
    lat = lambda m: m[0:bsz]
    cx = lambda m: jnp.broadcast_to(m[bsz:bsz + 1], (bsz, 1, D_MODEL))

    w = w_in[0]
    wz = w[:, 0:OFF_Z].astype(BF16)
    wxbc = w[:, OFF_Z:OFF_XBC].astype(BF16)
    wdt = w[:, OFF_XBC:OFF_DT]
    wdt_f = wdt[:, 0:N_HEADS][:, _HEAD_PERM]
    wdt_b = wdt[:, N_HEADS:2 * N_HEADS][:, _HEAD_PERM]
    wdt4 = jnp.concatenate([wdt_f, wdt_b, wdt_b, wdt_f], axis=1).astype(BF16)
    wglu_u = w[:, OFF_DT:OFF_DT + D_MODEL].astype(BF16)
    wglu_v = w[:, OFF_DT + D_MODEL:OFF_GLU].astype(BF16)
    wgates = w[:, OFF_GLU:].astype(BF16)
    perm2 = lambda p: jnp.concatenate([p[0][_HEAD_PERM], p[1][_HEAD_PERM],
                                       p[1][_HEAD_PERM], p[0][_HEAD_PERM]]).reshape(1, 2 * LANES)
    dtb4 = perm2(dt_bias[0])
    alog4 = perm2(a_log[0])
    cw = jnp.zeros((8, CONV_DIM), F32).at[0:SSM_CONV].set(ssm_conv_w[0])
    cb = ssm_conv_b[0].reshape(1, CONV_DIM)
    r3 = _expand_matrix([True, True, True, False])
    r2 = _expand_matrix([True, False, True, False])
    dsk = jnp.repeat(d_skip[0], HEAD_DIM).reshape(1, D_INNER)
    cfw = jnp.zeros((32, D_MODEL), F32).at[0:CF_KERNEL].set(cf_conv_w[0])

    hc = _normmod(ctx, norm_mix[0], cx(sc1), cx(sh1), tm=ctx_len)
    zero_state = jnp.zeros((bsz, D_STATE, D_INNER), F32)
    _, _, _, hf, hb = _front(hc, bsz, ctx_len, wxbc, wdt4, cw, cb, dtb4, alog4, r3, r2,
                             zero_state, zero_state)

    x2d = x.reshape(t, D_MODEL)
    hx = _normmod(x, norm_mix[0], lat(sc1), lat(sh1), tm=512)
    xbc, yf, yb, _, _ = _front(hx, bsz, seq, wxbc, wdt4, cw, cb, dtb4, alog4, r3, r2, hf, hb)
    sz = _mm(hx, wz, "silu", BF16)
    glu = _mm_glu(hx, wglu_u, wglu_v)
    sg = _mm(hx, wgates, "sigmoid", BF16)

    ya = _gate_proja(yf.reshape(t, D_INNER), yb.reshape(t, D_INNER), xbc.reshape(t, CONV_DIM), sz,
                     dsk, ssm_norm[0].reshape(1, D_INNER), w_proj_a[0].astype(BF16))
    cf = _cfconv(glu.reshape(bsz, seq, D_MODEL), cfw, cf_conv_b[0].reshape(1, D_MODEL))
    x1 = _merge(cf.reshape(t, D_MODEL), ya, sg, x2d, cf_ln_g[0].reshape(1, D_MODEL),
                cf_ln_b[0].reshape(1, D_MODEL), w_proj_b[0].astype(BF16), w_out[0].astype(BF16),
                lat(g1), seq)
    out = _ffn(x1, norm_ffn[0].reshape(1, D_MODEL), lat(sc2), lat(sh2), lat(g2),
               w_ffn_gate[0].astype(BF16), w_ffn_up[0].astype(BF16), w_ffn_down[0].astype(BF16),
               norm_final.reshape(1, D_MODEL), seq)
    return out.reshape(bsz, seq, D_MODEL)
```

```python
import jax
import jax.numpy as jnp
import numpy as np
from jax import lax
from jax.experimental import pallas as pl
from jax.experimental.pallas import tpu as pltpu

F32 = jnp.float32
BF16 = jnp.bfloat16

D_MODEL = 2048
GRID_W = 64
D_INNER = 2 * D_MODEL
HEAD_DIM = 64
N_HEADS = D_INNER // HEAD_DIM
N_GROUPS = 8
GROUP_W = D_INNER // N_GROUPS
D_STATE = 128
SSM_CONV = 5
CHUNK = 128
BC_DIM = N_GROUPS * D_STATE
CONV_DIM = D_INNER + 2 * BC_DIM
CF_KERNEL = 31
D_FF = 5632
EPS = 1e-6
OFF_Z = D_INNER
OFF_XBC = OFF_Z + CONV_DIM
OFF_DT = OFF_XBC + 2 * N_HEADS
OFF_GLU = OFF_DT + 2 * D_MODEL

LANES = 128
HALF = LANES // 2
MXU_N = 256
SUB_N = 2 * MXU_N
VMEM_LIMIT = 56 * 1024 * 1024
NEG_BIG = -1e30
HALO = 16
LOG2E = 1.4426950408889634


def _cparams(*sem):
    return pltpu.CompilerParams(dimension_semantics=sem, vmem_limit_bytes=VMEM_LIMIT)


def _sigmoid(v):
    return jax.nn.sigmoid(v)


def _silu(v):
    return v * jax.nn.sigmoid(v)


def _rms(v, w):
    return v * lax.rsqrt(jnp.mean(v * v, axis=-1, keepdims=True) + EPS) * w


def _ada_kernel(c_ref, w_ref, b_ref, o_ref):
    s = _silu(c_ref[...]).astype(BF16)
    o_ref[...] = jnp.dot(s, w_ref[...].astype(BF16), preferred_element_type=F32) + b_ref[...]


def _ada(cc, w_mod, b_mod):
    n = w_mod.shape[1]
    tn = 1024
    return pl.pallas_call(
        _ada_kernel,
        grid=(n // tn,),
        in_specs=[pl.BlockSpec((8, D_MODEL), lambda j: (0, 0)),
                  pl.BlockSpec((D_MODEL, tn), lambda j: (0, j)),
                  pl.BlockSpec((1, tn), lambda j: (0, j))],
        out_specs=pl.BlockSpec((8, tn), lambda j: (0, j)),
        out_shape=jax.ShapeDtypeStruct((8, n), F32),
        compiler_params=_cparams("arbitrary"),
        name="ada_params",
    )(cc, w_mod, b_mod.reshape(1, n))


def _normmod_kernel(xp_ref, x_ref, xn_ref, w_ref, sc_ref, sh_ref, o_ref):
    i = pl.program_id(1)
    n = pl.num_programs(1)
    tm = x_ref.shape[1]

    def nm(v):
        return _rms(v, w_ref[...]) * (1.0 + sc_ref[0]) + sh_ref[0]

    o_ref[0, 0:tm, :] = nm(x_ref[0]).astype(o_ref.dtype)
    nxt = jnp.where(i < n - 1, nm(xn_ref[0]), 0.0)
    prv = jnp.where(i > 0, nm(xp_ref[0]), 0.0)
    o_ref[0, tm:tm + HALO, :] = jnp.concatenate([nxt, prv], axis=0).astype(o_ref.dtype)


def _normmod(x3d, w, sc, sh, tm):
    bsz, seq, _ = x3d.shape
    nt = seq // tm
    r8 = tm // 8
    n8 = seq // 8
    mod = pl.BlockSpec((1, 1, D_MODEL), lambda b, i: (b, 0, 0))
    return pl.pallas_call(
        _normmod_kernel,
        grid=(bsz, nt),
        in_specs=[pl.BlockSpec((1, 8, D_MODEL), lambda b, i: (b, jnp.maximum(i * r8 - 1, 0), 0)),
                  pl.BlockSpec((1, tm, D_MODEL), lambda b, i: (b, i, 0)),
                  pl.BlockSpec((1, 8, D_MODEL), lambda b, i: (b, jnp.minimum((i + 1) * r8, n8 - 1), 0)),
                  pl.BlockSpec((1, D_MODEL), lambda b, i: (0, 0)), mod, mod],
        out_specs=pl.BlockSpec((1, tm + HALO, D_MODEL), lambda b, i: (b * nt + i, 0, 0)),
        out_shape=jax.ShapeDtypeStruct((bsz * nt, tm + HALO, D_MODEL), BF16),
        compiler_params=_cparams("arbitrary", "arbitrary"),
        name="norm_modulate",
    )(x3d, x3d, x3d, w.reshape(1, D_MODEL), sc, sh)


def _glu_kernel(a_ref, wu_ref, wv_ref, o_ref):
    a = a_ref[0]
    for nb in range(o_ref.shape[1] // SUB_N):
        cs = slice(nb * SUB_N, (nb + 1) * SUB_N)
        sv = _sigmoid(jnp.dot(a, wv_ref[:, cs], preferred_element_type=F32))
        u = jnp.dot(a, wu_ref[:, cs], preferred_element_type=F32)
        o_ref[:, cs] = (u * sv).astype(o_ref.dtype)


def _mm_glu(a3, wu, wv, tn=1024):
    ntile, tme, k = a3.shape
    tm = tme - HALO
    n = wu.shape[1]
    return pl.pallas_call(
        _glu_kernel,
        grid=(n // tn, ntile),
        in_specs=[pl.BlockSpec((1, tm, k), lambda j, i: (i, 0, 0)),
                  pl.BlockSpec((k, tn), lambda j, i: (0, j)),
                  pl.BlockSpec((k, tn), lambda j, i: (0, j))],
        out_specs=pl.BlockSpec((tm, tn), lambda j, i: (i, j)),
        out_shape=jax.ShapeDtypeStruct((ntile * tm, n), BF16),
        compiler_params=_cparams("arbitrary", "arbitrary"),
        name="proj_glu",
    )(a3, wu, wv)


def _zg_kernel(a_ref, wz_ref, wg_ref, oz_ref, og_ref):
    a = a_ref[0]
    for nb in range(oz_ref.shape[1] // SUB_N):
        cs = slice(nb * SUB_N, (nb + 1) * SUB_N)
        oz_ref[:, cs] = _silu(jnp.dot(a, wz_ref[:, cs], preferred_element_type=F32)).astype(oz_ref.dtype)
        og_ref[:, cs] = _sigmoid(jnp.dot(a, wg_ref[:, cs], preferred_element_type=F32)).astype(og_ref.dtype)


def _mm_zg(a3, wz, wg, tn=1024):
    ntile, tme, k = a3.shape
    tm = tme - HALO
    n = wz.shape[1]
    wspec = pl.BlockSpec((k, tn), lambda j, i: (0, j))
    ospec = pl.BlockSpec((tm, tn), lambda j, i: (i, j))
    shape = jax.ShapeDtypeStruct((ntile * tm, n), BF16)
    return pl.pallas_call(
        _zg_kernel,
        grid=(n // tn, ntile),
        in_specs=[pl.BlockSpec((1, tm, k), lambda j, i: (i, 0, 0)), wspec, wspec],
        out_specs=(ospec, ospec),
        out_shape=(shape, shape),
        compiler_params=_cparams("arbitrary", "arbitrary"),
        name="proj_z_gates",
    )(a3, wz, wg)


def _dt_kernel(a_ref, w_ref, bias_ref, alog_ref, dt_ref, da_ref):
    raw = jnp.dot(a_ref[0], w_ref[...], preferred_element_type=F32) + bias_ref[...]
    dt = jnp.maximum(raw, 0.0) + jnp.log1p(jnp.exp(-jnp.abs(raw)))
    dt_ref[...] = dt
    da_ref[...] = dt * (-jnp.exp(alog_ref[...])) * LOG2E


def _mm_dt(a3, w, bias, alog):
    ntile, tme, k = a3.shape
    tm = tme - HALO
    n = w.shape[1]
    vec = pl.BlockSpec((1, n), lambda i: (0, 0))
    out = pl.BlockSpec((tm, n), lambda i: (i, 0))
    shape = jax.ShapeDtypeStruct((ntile * tm, n), F32)
    return pl.pallas_call(
        _dt_kernel,
        grid=(ntile,),
        in_specs=[pl.BlockSpec((1, tm, k), lambda i: (i, 0, 0)),
                  pl.BlockSpec((k, n), lambda i: (0, 0)), vec, vec],
        out_specs=(out, out),
        out_shape=(shape, shape),
        compiler_params=_cparams("arbitrary"),
        name="proj_dt",
    )(a3, w, bias, alog)


def _xbc_kernel(a_ref, w_ref, cw_ref, cb_ref, o_ref):
    a = a_ref[0]
    n = a.shape[0]
    tm = n - HALO
    for nb in range(o_ref.shape[1] // MXU_N):
        cs = slice(nb * MXU_N, (nb + 1) * MXU_N)
        p = jnp.dot(a, w_ref[:, cs], preferred_element_type=F32)
        acc = cb_ref[:, cs] + cw_ref[2:3, cs] * p
        for k in (0, 1, 3, 4):
            acc = acc + cw_ref[k:k + 1, cs] * pltpu.roll(p, (2 - k) % n, 0)
        o_ref[:, cs] = _silu(acc[0:tm]).astype(o_ref.dtype)


def _mm_xbc(a3, w, cw, cb, tn=1024):
    ntile, tme, k = a3.shape
    tm = tme - HALO
    n = w.shape[1]
    return pl.pallas_call(
        _xbc_kernel,
        grid=(n // tn, ntile),
        in_specs=[pl.BlockSpec((1, tme, k), lambda j, i: (i, 0, 0)),
                  pl.BlockSpec((k, tn), lambda j, i: (0, j)),
                  pl.BlockSpec((8, tn), lambda j, i: (0, j)),
                  pl.BlockSpec((1, tn), lambda j, i: (0, j))],
        out_specs=pl.BlockSpec((tm, tn), lambda j, i: (i, j)),
        out_shape=jax.ShapeDtypeStruct((ntile * tm, n), BF16),
        compiler_params=_cparams("arbitrary", "arbitrary"),
        name="proj_xbc_conv",
    )(a3, w, cw, cb)


def _ssd_prologue(xs_ref, b_ref, c_ref, dt_ref, da_ref, r3_ref, r2_ref,
                  ecs_ref, xdt_ref, xw_ref, row_ref, cb_ref, reverse):
    q = CHUNK
    da = da_ref[0]
    ri = lax.broadcasted_iota(jnp.int32, (q, q), 0)
    ci = lax.broadcasted_iota(jnp.int32, (q, q), 1)
    tri = (ri <= ci) if reverse else (ri >= ci)
    cs = jnp.dot(tri.astype(F32), da, precision=lax.Precision.HIGHEST,
                 preferred_element_type=F32)
    cst = cs.T
    lane_h = lax.broadcasted_iota(jnp.int32, (N_HEADS // 2, LANES), 1) < HALF
    ev = cst[0:N_HEADS // 2]
    od = cst[N_HEADS // 2:N_HEADS]
    row_ref[0] = jnp.where(lane_h, ev, pltpu.roll(od, HALF, 1))
    row_ref[1] = jnp.where(lane_h, pltpu.roll(ev, HALF, 1), od)

    lane_q = lax.broadcasted_iota(jnp.int32, (q, LANES), 1)
    lo_half = lane_q < HALF
    hi = cs.astype(BF16).astype(F32)
    r1 = cs - hi
    mid = r1.astype(BF16).astype(F32)
    lo = (r1 - mid).astype(BF16).astype(F32)
    x1 = jnp.where(lo_half, hi, pltpu.roll(mid, HALF, 1))
    x2 = jnp.where(lo_half, lo, 0.0)
    cparts = jnp.concatenate([x1, x2], axis=1).astype(BF16)
    dt = dt_ref[0]
    dhi = dt.astype(BF16)
    dmid = (dt - dhi.astype(F32)).astype(BF16)
    dparts = jnp.concatenate([dhi, dmid], axis=1)

    last = 0 if reverse else q - 1
    for g in range(N_GROUPS):
        gs = slice(g * GROUP_W, (g + 1) * GROUP_W)
        ecs = jnp.dot(cparts, r3_ref[:, gs], preferred_element_type=F32)
        ecs_ref[:, gs] = ecs
        edt = jnp.dot(dparts, r2_ref[:, gs], preferred_element_type=F32)
        xdt = xs_ref[0, :, gs].astype(F32) * edt
        xdt_ref[:, gs] = xdt.astype(BF16)
        xw_ref[:, gs] = (xdt * jnp.exp2(ecs[last:last + 1, :] - ecs)).astype(BF16)

    for g in range(N_GROUPS):
        cg = c_ref[0, :, g * D_STATE:(g + 1) * D_STATE]
        bg = b_ref[0, :, g * D_STATE:(g + 1) * D_STATE]
        cb = lax.dot_general(cg, bg, (((1,), (1,)), ((), ())), preferred_element_type=F32)
        cbr = pltpu.roll(cb, HALF, 1)
        cb_ref[g, 0] = jnp.where(lo_half, cb, cbr)
        cb_ref[g, 1] = jnp.where(lo_half, cbr, cb)


def _ssd_main(b_ref, c_ref, y_ref, h_ref, ecs_ref, xdt_ref, xw_ref, row_ref, cb_ref, reverse):
    q = CHUNK
    last = 0 if reverse else q - 1
    lane_q = lax.broadcasted_iota(jnp.int32, (q, LANES), 1)
    lo_half = lane_q < HALF
    pos = lax.broadcasted_iota(jnp.int32, (q, LANES), 0)
    s0 = jnp.where(lo_half, lane_q, lane_q - HALF)
    s1 = s0 + HALF
    m0 = (pos <= s0) if reverse else (pos >= s0)
    m1 = (pos <= s1) if reverse else (pos >= s1)
    lo64 = lax.broadcasted_iota(jnp.int32, (HALF, LANES), 1) < HALF
    zero = jnp.zeros((HALF, LANES), BF16)

    for g in range(N_GROUPS):
        gs = slice(g * GROUP_W, (g + 1) * GROUP_W)
        cg = c_ref[0, :, g * D_STATE:(g + 1) * D_STATE]
        bg = b_ref[0, :, g * D_STATE:(g + 1) * D_STATE]
        cb0 = cb_ref[g, 0]
        cb1 = cb_ref[g, 1]
        hg = h_ref[0, :, gs]
        yoff = jnp.dot(cg, hg.astype(BF16), preferred_element_type=F32)
        for p in range(GROUP_W // LANES):
            k = g * (GROUP_W // LANES) + p
            sl = slice(k * LANES, (k + 1) * LANES)
            a = ecs_ref[:, sl]
            t0 = jnp.exp2(jnp.where(m0, a - row_ref[0, k:k + 1, :], NEG_BIG)) * cb0
            t1 = jnp.exp2(jnp.where(m1, a - row_ref[1, k:k + 1, :], NEG_BIG)) * cb1
            lhs = jnp.concatenate([t0, t1], axis=1).astype(BF16)
            xa = xdt_ref[0:HALF, sl]
            xb = xdt_ref[HALF:q, sl]
            rhs = jnp.concatenate([jnp.where(lo64, xa, zero), jnp.where(lo64, zero, xa),
                                   jnp.where(lo64, xb, zero), jnp.where(lo64, zero, xb)], axis=0)
            yd = jnp.dot(lhs, rhs, preferred_element_type=F32)
            y = yd + yoff[:, p * LANES:(p + 1) * LANES] * jnp.exp2(a)
            y_ref[0, :, sl] = y.astype(y_ref.dtype)
        st = lax.dot_general(bg, xw_ref[:, gs], (((0,), (0,)), ((), ())), preferred_element_type=F32)
        h_ref[0, :, gs] = hg * jnp.exp2(ecs_ref[last:last + 1, gs]) + st


def _ssd_kernel(xsf_ref, bf_ref, cf_ref, dtf_ref, daf_ref, xsb_ref, bb_ref, cbk_ref, dtb_ref, dab_ref,
                r3_ref, r2_ref, h0f_ref, h0b_ref, yf_ref, yb_ref, hf_ref, hb_ref,
                ecs_ref, xdt_ref, xw_ref, row_ref, cb_ref):
    @pl.when(pl.program_id(1) == 0)
    def _():
        hf_ref[...] = h0f_ref[...]
        hb_ref[...] = h0b_ref[...]

    sc = lambda d: (ecs_ref.at[d], xdt_ref.at[d], xw_ref.at[d], row_ref.at[d], cb_ref.at[d])
    _ssd_prologue(xsf_ref, bf_ref, cf_ref, dtf_ref, daf_ref, r3_ref, r2_ref, *sc(0), reverse=False)
    _ssd_prologue(xsb_ref, bb_ref, cbk_ref, dtb_ref, dab_ref, r3_ref, r2_ref, *sc(1), reverse=True)
    _ssd_main(bf_ref, cf_ref, yf_ref, hf_ref, *sc(0), reverse=False)
    _ssd_main(bb_ref, cbk_ref, yb_ref, hb_ref, *sc(1), reverse=True)


def _ssd(xbc, dt, da, r3, r2, h0f, h0b):
    bsz, seq, _ = xbc.shape
    nc = seq // CHUNK
    q = CHUNK
    bcol = D_INNER // BC_DIM
    fwd = lambda c: c
    bwd = lambda c: nc - 1 - c
    const = lambda r, c: pl.BlockSpec((r, c), lambda b, cc: (0, 0), pipeline_mode=pl.Buffered(1))
    state = pl.BlockSpec((1, D_STATE, D_INNER), lambda b, c: (b, 0, 0))

    def chunk_specs(ci, dcol):
        return [pl.BlockSpec((1, q, D_INNER), lambda b, c: (b, ci(c), 0)),
                pl.BlockSpec((1, q, BC_DIM), lambda b, c: (b, ci(c), bcol)),
                pl.BlockSpec((1, q, BC_DIM), lambda b, c: (b, ci(c), bcol + 1)),
                pl.BlockSpec((1, q, LANES), lambda b, c: (b, ci(c), dcol)),
                pl.BlockSpec((1, q, LANES), lambda b, c: (b, ci(c), dcol))]

    return pl.pallas_call(
        _ssd_kernel,
        grid=(bsz, nc),
        in_specs=chunk_specs(fwd, 0) + chunk_specs(bwd, 1)
        + [const(2 * LANES, D_INNER), const(2 * LANES, D_INNER), state, state],
        out_specs=(pl.BlockSpec((1, q, D_INNER), lambda b, c: (b, fwd(c), 0)),
                   pl.BlockSpec((1, q, D_INNER), lambda b, c: (b, bwd(c), 0)),
                   state, state),
        out_shape=(jax.ShapeDtypeStruct((bsz, seq, D_INNER), BF16),
                   jax.ShapeDtypeStruct((bsz, seq, D_INNER), BF16),
                   jax.ShapeDtypeStruct((bsz, D_STATE, D_INNER), F32),
                   jax.ShapeDtypeStruct((bsz, D_STATE, D_INNER), F32)),
        scratch_shapes=[pltpu.VMEM((2, q, D_INNER), F32),
                        pltpu.VMEM((2, q, D_INNER), BF16),
                        pltpu.VMEM((2, q, D_INNER), BF16),
                        pltpu.VMEM((2, 2, N_HEADS // 2, LANES), F32),
                        pltpu.VMEM((2, N_GROUPS, 2, q, LANES), F32)],
        compiler_params=_cparams("arbitrary", "arbitrary"),
        name="ssd_scan",
    )(xbc, xbc, xbc, dt, da, xbc, xbc, xbc, dt, da, r3, r2, h0f, h0b)


def _gate_proja_kernel(yf_ref, yb_ref, xs_ref, sz_ref, dsk_ref, nw_ref, w_ref, o_ref):
    y = (yf_ref[...].astype(F32) + yb_ref[...].astype(F32)
         + dsk_ref[...] * xs_ref[...].astype(F32))
    u = y * sz_ref[...].astype(F32)
    un = _rms(u, nw_ref[...]).astype(BF16)
    o_ref[...] = jnp.dot(un, w_ref[...], preferred_element_type=F32).astype(o_ref.dtype)


def _gate_proja(yf, yb, xbc, sz, dsk, nw, w, tm=256):
    t = yf.shape[0]
    tok = pl.BlockSpec((tm, D_INNER), lambda i: (i, 0))
    vec = pl.BlockSpec((1, D_INNER), lambda i: (0, 0))
    return pl.pallas_call(
        _gate_proja_kernel,
        grid=(t // tm,),
        in_specs=[tok, tok, tok, tok, vec, vec,
                  pl.BlockSpec((D_INNER, D_MODEL), lambda i: (0, 0), pipeline_mode=pl.Buffered(1))],
        out_specs=pl.BlockSpec((tm, D_MODEL), lambda i: (i, 0)),
        out_shape=jax.ShapeDtypeStruct((t, D_MODEL), BF16),
        compiler_params=_cparams("arbitrary"),
        name="gate_proj_a",
    )(yf, yb, xbc, sz, dsk, nw, w)


CF_TT = 1024
CF_CB = 512


def _cfconv_kernel(gp_ref, gc_ref, gn_ref, w_ref, b_ref, o_ref, win_ref):
    r = pl.program_id(2)
    n = pl.num_programs(2)
    win_ref[0:CF_TT, :] = jnp.where(r > 0, gp_ref[0].astype(F32), 0.0)
    win_ref[CF_TT:2 * CF_TT, :] = gc_ref[0].astype(F32)
    win_ref[2 * CF_TT:3 * CF_TT, :] = jnp.where(r < n - 1, gn_ref[0].astype(F32), 0.0)
    half = CF_KERNEL // 2

    def body(rr, carry):
        base = pl.multiple_of(CF_TT + rr * GRID_W - half * GRID_W, GRID_W)
        acc = jnp.zeros((GRID_W, CF_CB), F32) + b_ref[...]
        for k in range(CF_KERNEL):
            acc = acc + w_ref[k:k + 1, :] * win_ref[pl.ds(base + k * GRID_W, GRID_W), :]
        o_ref[0, pl.ds(pl.multiple_of(rr * GRID_W, GRID_W), GRID_W), :] = acc.astype(o_ref.dtype)
        return carry

    lax.fori_loop(0, CF_TT // GRID_W, body, 0)


def _cfconv(g, w, b):
    bsz, seq, ch = g.shape
    nt = seq // CF_TT
    blk = lambda f: pl.BlockSpec((1, CF_TT, CF_CB), f)
    return pl.pallas_call(
        _cfconv_kernel,
        grid=(bsz, ch // CF_CB, nt),
        in_specs=[blk(lambda bb, c, r: (bb, jnp.maximum(r - 1, 0), c)),
                  blk(lambda bb, c, r: (bb, r, c)),
                  blk(lambda bb, c, r: (bb, jnp.minimum(r + 1, nt - 1), c)),
                  pl.BlockSpec((32, CF_CB), lambda bb, c, r: (0, c)),
                  pl.BlockSpec((1, CF_CB), lambda bb, c, r: (0, c))],
        out_specs=blk(lambda bb, c, r: (bb, r, c)),
        out_shape=jax.ShapeDtypeStruct((bsz, seq, ch), BF16),
        scratch_shapes=[pltpu.VMEM((3 * CF_TT, CF_CB), F32)],
        compiler_params=_cparams("arbitrary", "arbitrary", "arbitrary"),
        name="conformer_conv",
    )(g, g, g, w, b)


def _merge_kernel(cf_ref, ya_ref, sg_ref, x_ref, lng_ref, lnb_ref, wpb_ref, wout_ref, g1_ref,
                  nw_ref, sc_ref, sh_ref, o_ref, hx_ref):
    cf = cf_ref[...].astype(F32)
    mu = jnp.mean(cf, axis=-1, keepdims=True)
    xc = cf - mu
    var = jnp.mean(xc * xc, axis=-1, keepdims=True)
    ln = xc * lax.rsqrt(var + EPS) * lng_ref[...] + lnb_ref[...]
    yb = jnp.dot(_silu(ln).astype(BF16), wpb_ref[...], preferred_element_type=F32)
    m = (sg_ref[:, 0:D_MODEL].astype(F32) * ya_ref[...].astype(F32)
         + sg_ref[:, D_MODEL:2 * D_MODEL].astype(F32) * yb)
    mix = jnp.dot(m.astype(BF16), wout_ref[...], preferred_element_type=F32)
    x1 = x_ref[...] + g1_ref[0] * mix
    o_ref[...] = x1
    hx_ref[...] = (_rms(x1, nw_ref[...]) * (1.0 + sc_ref[0]) + sh_ref[0]).astype(hx_ref.dtype)


def _merge(cf, ya, sg, x2d, lng, lnb, wpb, wout, g1, nw, sc2, sh2, tokens_per_batch, tm=256):
    t = x2d.shape[0]
    tpb = tokens_per_batch // tm
    tok = lambda w: pl.BlockSpec((tm, w), lambda i: (i, 0))
    vec = pl.BlockSpec((1, D_MODEL), lambda i: (0, 0))
    mod = pl.BlockSpec((1, 1, D_MODEL), lambda i: (i // tpb, 0, 0))
    wspec = pl.BlockSpec((D_MODEL, D_MODEL), lambda i: (0, 0), pipeline_mode=pl.Buffered(1))
    return pl.pallas_call(
        _merge_kernel,
        grid=(t // tm,),
        in_specs=[tok(D_MODEL), tok(D_MODEL), tok(2 * D_MODEL), tok(D_MODEL), vec, vec, wspec, wspec,
                  mod, vec, mod, mod],
        out_specs=(tok(D_MODEL), tok(D_MODEL)),
        out_shape=(jax.ShapeDtypeStruct((t, D_MODEL), F32), jax.ShapeDtypeStruct((t, D_MODEL), BF16)),
        compiler_params=_cparams("arbitrary"),
        name="merge_out_proj",
    )(cf, ya, sg, x2d, lng, lnb, wpb, wout, g1, nw, sc2, sh2)


def _ffn_up_kernel(a_ref, wg_ref, wu_ref, o_ref):
    a = a_ref[...]
    gate = jnp.dot(a, wg_ref[...], preferred_element_type=F32)
    up = jnp.dot(a, wu_ref[...], preferred_element_type=F32)
    o_ref[...] = (_silu(gate) * up).astype(o_ref.dtype)


def _ffn_up(hx, wg, wu, tm=1024, tn=512):
    t, k = hx.shape
    n = wg.shape[1]
    return pl.pallas_call(
        _ffn_up_kernel,
        grid=(n // tn, t // tm),
        in_specs=[pl.BlockSpec((tm, k), lambda j, i: (i, 0)),
                  pl.BlockSpec((k, tn), lambda j, i: (0, j)),
                  pl.BlockSpec((k, tn), lambda j, i: (0, j))],
        out_specs=pl.BlockSpec((tm, tn), lambda j, i: (i, j)),
        out_shape=jax.ShapeDtypeStruct((t, n), BF16),
        compiler_params=_cparams("arbitrary", "arbitrary"),
        name="ffn_up",
    )(hx, wg, wu)


def _ffn_down_kernel(h_ref, w_ref, x_ref, g2_ref, nf_ref, o_ref):
    acc = jnp.dot(h_ref[...], w_ref[...], preferred_element_type=F32)
    o_ref[...] = _rms(x_ref[...] + g2_ref[0] * acc, nf_ref[...])


def _ffn_down(h1, wd, x1, g2, nf, tokens_per_batch, tm=256):
    t = x1.shape[0]
    tpb = tokens_per_batch // tm
    return pl.pallas_call(
        _ffn_down_kernel,
        grid=(t // tm,),
        in_specs=[pl.BlockSpec((tm, D_FF), lambda i: (i, 0)),
                  pl.BlockSpec((D_FF, D_MODEL), lambda i: (0, 0), pipeline_mode=pl.Buffered(1)),
                  pl.BlockSpec((tm, D_MODEL), lambda i: (i, 0)),
                  pl.BlockSpec((1, 1, D_MODEL), lambda i: (i // tpb, 0, 0)),
                  pl.BlockSpec((1, D_MODEL), lambda i: (0, 0))],
        out_specs=pl.BlockSpec((tm, D_MODEL), lambda i: (i, 0)),
        out_shape=jax.ShapeDtypeStruct((t, D_MODEL), F32),
        compiler_params=_cparams("arbitrary"),
        name="ffn_down",
    )(h1, wd, x1, g2, nf)


_HEAD_PERM = np.concatenate([np.arange(0, N_HEADS, 2), np.arange(1, N_HEADS, 2)])


def _expand_matrix(row_blocks):
    col_head = np.arange(D_INNER) // HEAD_DIM
    pat = (_HEAD_PERM[:, None] == col_head[None, :]).astype(np.float32)
    zero = np.zeros_like(pat)
    return jnp.asarray(np.concatenate([pat if on else zero for on in row_blocks], axis=0), BF16)


def _front(h3, bsz, seq, wxbc, wdt4, cw, cb, dtb4, alog4, r3, r2, h0f, h0b):
    xbc = _mm_xbc(h3, wxbc, cw, cb).reshape(bsz, seq, CONV_DIM)
    dt, da = _mm_dt(h3, wdt4, dtb4, alog4)
    dt = dt.reshape(bsz, seq, 2 * LANES)
    da = da.reshape(bsz, seq, 2 * LANES)
    yf, yb, hf, hb = _ssd(xbc, dt, da, r3, r2, h0f, h0b)
    return xbc, yf, yb, hf, hb


@jax.jit
def kernel(x, c, ctx, c_ctx, w_mod, b_mod, norm_mix, w_in, ssm_conv_w, ssm_conv_b, dt_bias, a_log,
           d_skip, ssm_norm, cf_conv_w, cf_conv_b, cf_ln_g, cf_ln_b, w_proj_a, w_proj_b, w_out,
           norm_ffn, w_ffn_gate, w_ffn_up, w_ffn_down, norm_final):
    bsz, seq, _ = x.shape
    ctx_len = ctx.shape[1]
    t = bsz * seq

    cc = jnp.zeros((8, D_MODEL), F32).at[0:bsz].set(c).at[bsz].set(c_ctx)
    mod = _ada(cc, w_mod[0], b_mod[0])
    sh1, sc1, g1, sh2, sc2, g2 = [m.reshape(8, 1, D_MODEL) for m in jnp.split(mod, 6, axis=-1)]
    lat = lambda m: m[0:bsz]
    cx = lambda m: jnp.broadcast_to(m[bsz:bsz + 1], (bsz, 1, D_MODEL))

    w = w_in[0]
    wz = w[:, 0:OFF_Z].astype(BF16)
    wxbc = w[:, OFF_Z:OFF_XBC].astype(BF16)
    wdt = w[:, OFF_XBC:OFF_DT]
    wdt_f = wdt[:, 0:N_HEADS][:, _HEAD_PERM]
    wdt_b = wdt[:, N_HEADS:2 * N_HEADS][:, _HEAD_PERM]
    wdt4 = jnp.concatenate([wdt_f, wdt_b, wdt_b, wdt_f], axis=1).astype(BF16)
    wglu_u = w[:, OFF_DT:OFF_DT + D_MODEL].astype(BF16)
    wglu_v = w[:, OFF_DT + D_MODEL:OFF_GLU].astype(BF16)
    wgates = w[:, OFF_GLU:].astype(BF16)
    perm2 = lambda p: jnp.concatenate([p[0][_HEAD_PERM], p[1][_HEAD_PERM],
                                       p[1][_HEAD_PERM], p[0][_HEAD_PERM]]).reshape(1, 2 * LANES)
    dtb4 = perm2(dt_bias[0])
    alog4 = perm2(a_log[0])
    cw = jnp.zeros((8, CONV_DIM), F32).at[0:SSM_CONV].set(ssm_conv_w[0])
    cb = ssm_conv_b[0].reshape(1, CONV_DIM)
    r3 = _expand_matrix([True, True, True, False])
    r2 = _expand_matrix([True, False, True, False])
    dsk = jnp.repeat(d_skip[0], HEAD_DIM).reshape(1, D_INNER)
    cfw = jnp.zeros((32, D_MODEL), F32).at[0:CF_KERNEL].set(cf_conv_w[0])

    hc = _normmod(ctx, norm_mix[0], cx(sc1), cx(sh1), tm=ctx_len)
    zero_state = jnp.zeros((bsz, D_STATE, D_INNER), F32)
    _, _, _, hf, hb = _front(hc, bsz, ctx_len, wxbc, wdt4, cw, cb, dtb4, alog4, r3, r2,
                             zero_state, zero_state)

    x2d = x.reshape(t, D_MODEL)
    hx = _normmod(x, norm_mix[0], lat(sc1), lat(sh1), tm=1024)
    xbc, yf, yb, _, _ = _front(hx, bsz, seq, wxbc, wdt4, cw, cb, dtb4, alog4, r3, r2, hf, hb)
    sz, sg = _mm_zg(hx, wz, wgates)
    glu = _mm_glu(hx, wglu_u, wglu_v)

    ya = _gate_proja(yf.reshape(t, D_INNER), yb.reshape(t, D_INNER), xbc.reshape(t, CONV_DIM), sz,
                     dsk, ssm_norm[0].reshape(1, D_INNER), w_proj_a[0].astype(BF16))
    cf = _cfconv(glu.reshape(bsz, seq, D_MODEL), cfw, cf_conv_b[0].reshape(1, D_MODEL))
    x1, hx2 = _merge(cf.reshape(t, D_MODEL), ya, sg, x2d, cf_ln_g[0].reshape(1, D_MODEL),
                     cf_ln_b[0].reshape(1, D_MODEL), w_proj_b[0].astype(BF16), w_out[0].astype(BF16),
                     lat(g1), norm_ffn[0].reshape(1, D_MODEL), lat(sc2), lat(sh2), seq)
    h1 = _ffn_up(hx2, w_ffn_gate[0].astype(BF16), w_ffn_up[0].astype(BF16))
    out = _ffn_down(h1, w_ffn_down[0].astype(BF16), x1, lat(g2), norm_final.reshape(1, D_MODEL), seq)
    return out.reshape(bsz, seq, D_MODEL)
```

```python
import jax
import jax.numpy as jnp
import numpy as np
from jax import lax
from jax.experimental import pallas as pl
from jax.experimental.pallas import tpu as pltpu

F32 = jnp.float32
BF16 = jnp.bfloat16

D_MODEL = 2048
GRID_W = 64
D_INNER = 2 * D_MODEL
HEAD_DIM = 64
N_HEADS = D_INNER // HEAD_DIM
N_GROUPS = 8
GROUP_W = D_INNER // N_GROUPS
D_STATE = 128
SSM_CONV = 5
CHUNK = 128
BC_DIM = N_GROUPS * D_STATE
CONV_DIM = D_INNER + 2 * BC_DIM
CF_KERNEL = 31
D_FF = 5632
EPS = 1e-6
OFF_Z = D_INNER
OFF_XBC = OFF_Z + CONV_DIM
OFF_DT = OFF_XBC + 2 * N_HEADS
OFF_GLU = OFF_DT + 2 * D_MODEL

LANES = 128
HALF = LANES // 2
MXU_N = 256
SUB_N = 2 * MXU_N
VMEM_LIMIT = 56 * 1024 * 1024
NEG_BIG = -1e30
HALO = 16
LOG2E = 1.4426950408889634


def _cparams(*sem):
    return pltpu.CompilerParams(dimension_semantics=sem, vmem_limit_bytes=VMEM_LIMIT)


def _sigmoid(v):
    return jax.nn.sigmoid(v)


def _silu(v):
    return v * jax.nn.sigmoid(v)


def _rms(v, w):
    return v * lax.rsqrt(jnp.mean(v * v, axis=-1, keepdims=True) + EPS) * w


def _ada_kernel(c_ref, w_ref, b_ref, o_ref):
    s = _silu(c_ref[...]).astype(BF16)
    o_ref[...] = jnp.dot(s, w_ref[...].astype(BF16), preferred_element_type=F32) + b_ref[...]


def _ada(cc, w_mod, b_mod):
    n = w_mod.shape[1]
    tn = 1024
    return pl.pallas_call(
        _ada_kernel,
        grid=(n // tn,),
        in_specs=[pl.BlockSpec((8, D_MODEL), lambda j: (0, 0)),
                  pl.BlockSpec((D_MODEL, tn), lambda j: (0, j)),
                  pl.BlockSpec((1, tn), lambda j: (0, j))],
        out_specs=pl.BlockSpec((8, tn), lambda j: (0, j)),
        out_shape=jax.ShapeDtypeStruct((8, n), F32),
        compiler_params=_cparams("arbitrary"),
        name="ada_params",
    )(cc, w_mod, b_mod.reshape(1, n))


def _normmod_kernel(xp_ref, x_ref, xn_ref, w_ref, sc_ref, sh_ref, o_ref):
    i = pl.program_id(1)
    n = pl.num_programs(1)
    tm = x_ref.shape[1]

    def nm(v):
        return _rms(v, w_ref[...]) * (1.0 + sc_ref[0]) + sh_ref[0]

    o_ref[0, 0:tm, :] = nm(x_ref[0]).astype(o_ref.dtype)
    nxt = jnp.where(i < n - 1, nm(xn_ref[0]), 0.0)
    prv = jnp.where(i > 0, nm(xp_ref[0]), 0.0)
    o_ref[0, tm:tm + HALO, :] = jnp.concatenate([nxt, prv], axis=0).astype(o_ref.dtype)


def _normmod(x3d, w, sc, sh, tm):
    bsz, seq, _ = x3d.shape
    nt = seq // tm
    r8 = tm // 8
    n8 = seq // 8
    mod = pl.BlockSpec((1, 1, D_MODEL), lambda b, i: (b, 0, 0))
    return pl.pallas_call(
        _normmod_kernel,
        grid=(bsz, nt),
        in_specs=[pl.BlockSpec((1, 8, D_MODEL), lambda b, i: (b, jnp.maximum(i * r8 - 1, 0), 0)),
                  pl.BlockSpec((1, tm, D_MODEL), lambda b, i: (b, i, 0)),
                  pl.BlockSpec((1, 8, D_MODEL), lambda b, i: (b, jnp.minimum((i + 1) * r8, n8 - 1), 0)),
                  pl.BlockSpec((1, D_MODEL), lambda b, i: (0, 0)), mod, mod],
        out_specs=pl.BlockSpec((1, tm + HALO, D_MODEL), lambda b, i: (b * nt + i, 0, 0)),
        out_shape=jax.ShapeDtypeStruct((bsz * nt, tm + HALO, D_MODEL), BF16),
        compiler_params=_cparams("arbitrary", "arbitrary"),
        name="norm_modulate",
    )(x3d, x3d, x3d, w.reshape(1, D_MODEL), sc, sh)


def _glu_kernel(a_ref, wu_ref, wv_ref, o_ref):
    a = a_ref[0]
    for nb in range(o_ref.shape[1] // SUB_N):
        cs = slice(nb * SUB_N, (nb + 1) * SUB_N)
        sv = _sigmoid(jnp.dot(a, wv_ref[:, cs], preferred_element_type=F32))
        u = jnp.dot(a, wu_ref[:, cs], preferred_element_type=F32)
        o_ref[:, cs] = (u * sv).astype(o_ref.dtype)


def _mm_glu(a3, wu, wv, tn=1024):
    ntile, tme, k = a3.shape
    tm = tme - HALO
    n = wu.shape[1]
    return pl.pallas_call(
        _glu_kernel,
        grid=(n // tn, ntile),
        in_specs=[pl.BlockSpec((1, tm, k), lambda j, i: (i, 0, 0)),
                  pl.BlockSpec((k, tn), lambda j, i: (0, j)),
                  pl.BlockSpec((k, tn), lambda j, i: (0, j))],
        out_specs=pl.BlockSpec((tm, tn), lambda j, i: (i, j)),
        out_shape=jax.ShapeDtypeStruct((ntile * tm, n), BF16),
        compiler_params=_cparams("arbitrary", "arbitrary"),
        name="proj_glu",
    )(a3, wu, wv)


def _zg_kernel(a_ref, wz_ref, wg_ref, oz_ref, og_ref):
    a = a_ref[0]
    for nb in range(oz_ref.shape[1] // SUB_N):
        cs = slice(nb * SUB_N, (nb + 1) * SUB_N)
        oz_ref[:, cs] = _silu(jnp.dot(a, wz_ref[:, cs], preferred_element_type=F32)).astype(oz_ref.dtype)
        og_ref[:, cs] = _sigmoid(jnp.dot(a, wg_ref[:, cs], preferred_element_type=F32)).astype(og_ref.dtype)


def _mm_zg(a3, wz, wg, tn=1024):
    ntile, tme, k = a3.shape
    tm = tme - HALO
    n = wz.shape[1]
    wspec = pl.BlockSpec((k, tn), lambda j, i: (0, j))
    ospec = pl.BlockSpec((tm, tn), lambda j, i: (i, j))
    shape = jax.ShapeDtypeStruct((ntile * tm, n), BF16)
    return pl.pallas_call(
        _zg_kernel,
        grid=(n // tn, ntile),
        in_specs=[pl.BlockSpec((1, tm, k), lambda j, i: (i, 0, 0)), wspec, wspec],
        out_specs=(ospec, ospec),
        out_shape=(shape, shape),
        compiler_params=_cparams("arbitrary", "arbitrary"),
        name="proj_z_gates",
    )(a3, wz, wg)


def _dt_kernel(a_ref, w_ref, bias_ref, alog_ref, dt_ref, da_ref):
    raw = jnp.dot(a_ref[0], w_ref[...], preferred_element_type=F32) + bias_ref[...]
    dt = jnp.maximum(raw, 0.0) + jnp.log1p(jnp.exp(-jnp.abs(raw)))
    dt_ref[...] = dt
    da_ref[...] = dt * (-jnp.exp(alog_ref[...])) * LOG2E


def _mm_dt(a3, w, bias, alog):
    ntile, tme, k = a3.shape
    tm = tme - HALO
    n = w.shape[1]
    vec = pl.BlockSpec((1, n), lambda i: (0, 0))
    out = pl.BlockSpec((tm, n), lambda i: (i, 0))
    shape = jax.ShapeDtypeStruct((ntile * tm, n), F32)
    return pl.pallas_call(
        _dt_kernel,
        grid=(ntile,),
        in_specs=[pl.BlockSpec((1, tm, k), lambda i: (i, 0, 0)),
                  pl.BlockSpec((k, n), lambda i: (0, 0)), vec, vec],
        out_specs=(out, out),
        out_shape=(shape, shape),
        compiler_params=_cparams("arbitrary"),
        name="proj_dt",
    )(a3, w, bias, alog)


def _xbc_kernel(a_ref, w_ref, cw_ref, cb_ref, o_ref):
    a = a_ref[0]
    n = a.shape[0]
    tm = n - HALO
    for nb in range(o_ref.shape[1] // MXU_N):
        cs = slice(nb * MXU_N, (nb + 1) * MXU_N)
        p = jnp.dot(a, w_ref[:, cs], preferred_element_type=F32)
        acc = cb_ref[:, cs] + cw_ref[2:3, cs] * p
        for k in (0, 1, 3, 4):
            acc = acc + cw_ref[k:k + 1, cs] * pltpu.roll(p, (2 - k) % n, 0)
        o_ref[:, cs] = _silu(acc[0:tm]).astype(o_ref.dtype)


def _mm_xbc(a3, w, cw, cb, tn=1024):
    ntile, tme, k = a3.shape
    tm = tme - HALO
    n = w.shape[1]
    return pl.pallas_call(
        _xbc_kernel,
        grid=(n // tn, ntile),
        in_specs=[pl.BlockSpec((1, tme, k), lambda j, i: (i, 0, 0)),
                  pl.BlockSpec((k, tn), lambda j, i: (0, j)),
                  pl.BlockSpec((8, tn), lambda j, i: (0, j)),
                  pl.BlockSpec((1, tn), lambda j, i: (0, j))],
        out_specs=pl.BlockSpec((tm, tn), lambda j, i: (i, j)),
        out_shape=jax.ShapeDtypeStruct((ntile * tm, n), BF16),
        compiler_params=_cparams("arbitrary", "arbitrary"),
        name="proj_xbc_conv",
    )(a3, w, cw, cb)


def _ssd_prologue(xs_ref, b_ref, c_ref, dt_ref, da_ref, r3_ref, r2_ref,
                  ecs_ref, xdt_ref, xw_ref, row_ref, cb_ref, reverse):
    q = CHUNK
    da = da_ref[0]
    ri = lax.broadcasted_iota(jnp.int32, (q, q), 0)
    ci = lax.broadcasted_iota(jnp.int32, (q, q), 1)
    tri = (ri <= ci) if reverse else (ri >= ci)
    cs = jnp.dot(tri.astype(F32), da, precision=lax.Precision.HIGHEST,
                 preferred_element_type=F32)
    cst = cs.T
    lane_h = lax.broadcasted_iota(jnp.int32, (N_HEADS // 2, LANES), 1) < HALF
    ev = cst[0:N_HEADS // 2]
    od = cst[N_HEADS // 2:N_HEADS]
    row_ref[0] = jnp.where(lane_h, ev, pltpu.roll(od, HALF, 1))
    row_ref[1] = jnp.where(lane_h, pltpu.roll(ev, HALF, 1), od)

    lane_q = lax.broadcasted_iota(jnp.int32, (q, LANES), 1)
    lo_half = lane_q < HALF
    hi = cs.astype(BF16).astype(F32)
    r1 = cs - hi
    mid = r1.astype(BF16).astype(F32)
    lo = (r1 - mid).astype(BF16).astype(F32)
    x1 = jnp.where(lo_half, hi, pltpu.roll(mid, HALF, 1))
    x2 = jnp.where(lo_half, lo, 0.0)
    cparts = jnp.concatenate([x1, x2], axis=1).astype(BF16)
    dt = dt_ref[0]
    dhi = dt.astype(BF16)
    dmid = (dt - dhi.astype(F32)).astype(BF16)
    dparts = jnp.concatenate([dhi, dmid], axis=1)

    last = 0 if reverse else q - 1
    for g in range(N_GROUPS):
        gs = slice(g * GROUP_W, (g + 1) * GROUP_W)
        ecs = jnp.dot(cparts, r3_ref[:, gs], preferred_element_type=F32)
        ecs_ref[:, gs] = ecs
        edt = jnp.dot(dparts, r2_ref[:, gs], preferred_element_type=F32)
        xdt = xs_ref[0, :, gs].astype(F32) * edt
        xdt_ref[:, gs] = xdt.astype(BF16)
        xw_ref[:, gs] = (xdt * jnp.exp2(ecs[last:last + 1, :] - ecs)).astype(BF16)

    for g in range(N_GROUPS):
        cg = c_ref[0, :, g * D_STATE:(g + 1) * D_STATE]
        bg = b_ref[0, :, g * D_STATE:(g + 1) * D_STATE]
        cb = lax.dot_general(cg, bg, (((1,), (1,)), ((), ())), preferred_element_type=F32)
        cbr = pltpu.roll(cb, HALF, 1)
        cb_ref[g, 0] = jnp.where(lo_half, cb, cbr)
        cb_ref[g, 1] = jnp.where(lo_half, cbr, cb)


def _ssd_main(b_ref, c_ref, y_ref, h_ref, ecs_ref, xdt_ref, xw_ref, row_ref, cb_ref, reverse):
    q = CHUNK
    last = 0 if reverse else q - 1
    lane_q = lax.broadcasted_iota(jnp.int32, (q, LANES), 1)
    lo_half = lane_q < HALF
    pos = lax.broadcasted_iota(jnp.int32, (q, LANES), 0)
    s0 = jnp.where(lo_half, lane_q, lane_q - HALF)
    s1 = s0 + HALF
    m0 = (pos <= s0) if reverse else (pos >= s0)
    m1 = (pos <= s1) if reverse else (pos >= s1)
    lo64 = lax.broadcasted_iota(jnp.int32, (HALF, LANES), 1) < HALF
    zero = jnp.zeros((HALF, LANES), BF16)

    for g in range(N_GROUPS):
        gs = slice(g * GROUP_W, (g + 1) * GROUP_W)
        cg = c_ref[0, :, g * D_STATE:(g + 1) * D_STATE]
        bg = b_ref[0, :, g * D_STATE:(g + 1) * D_STATE]
        cb0 = cb_ref[g, 0]
        cb1 = cb_ref[g, 1]
        hg = h_ref[0, :, gs]
        yoff = jnp.dot(cg, hg.astype(BF16), preferred_element_type=F32)
        for p in range(GROUP_W // LANES):
            k = g * (GROUP_W // LANES) + p
            sl = slice(k * LANES, (k + 1) * LANES)
            a = ecs_ref[:, sl]
            t0 = jnp.exp2(jnp.where(m0, a - row_ref[0, k:k + 1, :], NEG_BIG)) * cb0
            t1 = jnp.exp2(jnp.where(m1, a - row_ref[1, k:k + 1, :], NEG_BIG)) * cb1
            lhs = jnp.concatenate([t0, t1], axis=1).astype(BF16)
            xa = xdt_ref[0:HALF, sl]
            xb = xdt_ref[HALF:q, sl]
            rhs = jnp.concatenate([jnp.where(lo64, xa, zero), jnp.where(lo64, zero, xa),
                                   jnp.where(lo64, xb, zero), jnp.where(lo64, zero, xb)], axis=0)
            yd = jnp.dot(lhs, rhs, preferred_element_type=F32)
            y = yd + yoff[:, p * LANES:(p + 1) * LANES] * jnp.exp2(a)
            y_ref[0, :, sl] = y.astype(y_ref.dtype)
        st = lax.dot_general(bg, xw_ref[:, gs], (((0,), (0,)), ((), ())), preferred_element_type=F32)
        h_ref[0, :, gs] = hg * jnp.exp2(ecs_ref[last:last + 1, gs]) + st


def _ssd_kernel(xsf_ref, bf_ref, cf_ref, dtf_ref, daf_ref, xsb_ref, bb_ref, cbk_ref, dtb_ref, dab_ref,
                r3_ref, r2_ref, h0f_ref, h0b_ref, yf_ref, yb_ref, hf_ref, hb_ref,
                ecs_ref, xdt_ref, xw_ref, row_ref, cb_ref):
    @pl.when(pl.program_id(1) == 0)
    def _():
        hf_ref[...] = h0f_ref[...]
        hb_ref[...] = h0b_ref[...]

    sc = lambda d: (ecs_ref.at[d], xdt_ref.at[d], xw_ref.at[d], row_ref.at[d], cb_ref.at[d])
    _ssd_prologue(xsf_ref, bf_ref, cf_ref, dtf_ref, daf_ref, r3_ref, r2_ref, *sc(0), reverse=False)
    _ssd_prologue(xsb_ref, bb_ref, cbk_ref, dtb_ref, dab_ref, r3_ref, r2_ref, *sc(1), reverse=True)
    _ssd_main(bf_ref, cf_ref, yf_ref, hf_ref, *sc(0), reverse=False)
    _ssd_main(bb_ref, cbk_ref, yb_ref, hb_ref, *sc(1), reverse=True)


def _ssd(xbc, dt, da, r3, r2, h0f, h0b):
    bsz, seq, _ = xbc.shape
    nc = seq // CHUNK
    q = CHUNK
    bcol = D_INNER // BC_DIM
    fwd = lambda c: c
    bwd = lambda c: nc - 1 - c
    const = lambda r, c: pl.BlockSpec((r, c), lambda b, cc: (0, 0), pipeline_mode=pl.Buffered(1))
    state = pl.BlockSpec((1, D_STATE, D_INNER), lambda b, c: (b, 0, 0))

    def chunk_specs(ci, dcol):
        return [pl.BlockSpec((1, q, D_INNER), lambda b, c: (b, ci(c), 0)),
                pl.BlockSpec((1, q, BC_DIM), lambda b, c: (b, ci(c), bcol)),
                pl.BlockSpec((1, q, BC_DIM), lambda b, c: (b, ci(c), bcol + 1)),
                pl.BlockSpec((1, q, LANES), lambda b, c: (b, ci(c), dcol)),
                pl.BlockSpec((1, q, LANES), lambda b, c: (b, ci(c), dcol))]

    return pl.pallas_call(
        _ssd_kernel,
        grid=(bsz, nc),
        in_specs=chunk_specs(fwd, 0) + chunk_specs(bwd, 1)
        + [const(2 * LANES, D_INNER), const(2 * LANES, D_INNER), state, state],
        out_specs=(pl.BlockSpec((1, q, D_INNER), lambda b, c: (b, fwd(c), 0)),
                   pl.BlockSpec((1, q, D_INNER), lambda b, c: (b, bwd(c), 0)),
                   state, state),
        out_shape=(jax.ShapeDtypeStruct((bsz, seq, D_INNER), BF16),
                   jax.ShapeDtypeStruct((bsz, seq, D_INNER), BF16),
                   jax.ShapeDtypeStruct((bsz, D_STATE, D_INNER), F32),
                   jax.ShapeDtypeStruct((bsz, D_STATE, D_INNER), F32)),
        scratch_shapes=[pltpu.VMEM((2, q, D_INNER), F32),
                        pltpu.VMEM((2, q, D_INNER), BF16),
                        pltpu.VMEM((2, q, D_INNER), BF16),
                        pltpu.VMEM((2, 2, N_HEADS // 2, LANES), F32),
                        pltpu.VMEM((2, N_GROUPS, 2, q, LANES), F32)],
        compiler_params=_cparams("arbitrary", "arbitrary"),
        name="ssd_scan",
    )(xbc, xbc, xbc, dt, da, xbc, xbc, xbc, dt, da, r3, r2, h0f, h0b)


def _gate_proja_kernel(yf_ref, yb_ref, xs_ref, sz_ref, dsk_ref, nw_ref, w_ref, o_ref):
    y = (yf_ref[...].astype(F32) + yb_ref[...].astype(F32)
         + dsk_ref[...] * xs_ref[...].astype(F32))
    u = y * sz_ref[...].astype(F32)
    un = _rms(u, nw_ref[...]).astype(BF16)
    o_ref[...] = jnp.dot(un, w_ref[...], preferred_element_type=F32).astype(o_ref.dtype)


def _gate_proja(yf, yb, xbc, sz, dsk, nw, w, tm=256):
    t = yf.shape[0]
    tok = pl.BlockSpec((tm, D_INNER), lambda i: (i, 0))
    vec = pl.BlockSpec((1, D_INNER), lambda i: (0, 0))
    return pl.pallas_call(
        _gate_proja_kernel,
        grid=(t // tm,),
        in_specs=[tok, tok, tok, tok, vec, vec,
                  pl.BlockSpec((D_INNER, D_MODEL), lambda i: (0, 0), pipeline_mode=pl.Buffered(1))],
        out_specs=pl.BlockSpec((tm, D_MODEL), lambda i: (i, 0)),
        out_shape=jax.ShapeDtypeStruct((t, D_MODEL), BF16),
        compiler_params=_cparams("arbitrary"),
        name="gate_proj_a",
    )(yf, yb, xbc, sz, dsk, nw, w)


CF_TT = 2048
CF_HALO = 1024
CF_CB = 256


def _cfconv_kernel(gp_ref, gc_ref, gn_ref, w_ref, b_ref, o_ref, win_ref):
    r = pl.program_id(2)
    n = pl.num_programs(2)
    win_ref[0:CF_HALO, :] = jnp.where(r > 0, gp_ref[0].astype(F32), 0.0)
    win_ref[CF_HALO:CF_HALO + CF_TT, :] = gc_ref[0].astype(F32)
    win_ref[CF_HALO + CF_TT:2 * CF_HALO + CF_TT, :] = jnp.where(r < n - 1, gn_ref[0].astype(F32), 0.0)
    half = CF_KERNEL // 2

    def body(rr, carry):
        base = pl.multiple_of(CF_HALO + 2 * rr * GRID_W - half * GRID_W, GRID_W)
        acc0 = jnp.zeros((GRID_W, CF_CB), F32) + b_ref[...]
        acc1 = acc0
        for j in range(CF_KERNEL + 1):
            xrow = win_ref[pl.ds(base + j * GRID_W, GRID_W), :]
            if j < CF_KERNEL:
                acc0 = acc0 + w_ref[j:j + 1, :] * xrow
            if j > 0:
                acc1 = acc1 + w_ref[j - 1:j, :] * xrow
        out0 = pl.multiple_of(2 * rr * GRID_W, GRID_W)
        o_ref[0, pl.ds(out0, GRID_W), :] = acc0.astype(o_ref.dtype)
        o_ref[0, pl.ds(pl.multiple_of(out0 + GRID_W, GRID_W), GRID_W), :] = acc1.astype(o_ref.dtype)
        return carry

    lax.fori_loop(0, CF_TT // (2 * GRID_W), body, 0)


def _cfconv(g, w, b):
    bsz, seq, ch = g.shape
    nt = seq // CF_TT
    hpt = CF_TT // CF_HALO
    nh = seq // CF_HALO
    blk = lambda f: pl.BlockSpec((1, CF_TT, CF_CB), f)
    hblk = lambda f: pl.BlockSpec((1, CF_HALO, CF_CB), f)
    return pl.pallas_call(
        _cfconv_kernel,
        grid=(bsz, ch // CF_CB, nt),
        in_specs=[hblk(lambda bb, c, r: (bb, jnp.maximum(r * hpt - 1, 0), c)),
                  blk(lambda bb, c, r: (bb, r, c)),
                  hblk(lambda bb, c, r: (bb, jnp.minimum((r + 1) * hpt, nh - 1), c)),
                  pl.BlockSpec((32, CF_CB), lambda bb, c, r: (0, c)),
                  pl.BlockSpec((1, CF_CB), lambda bb, c, r: (0, c))],
        out_specs=blk(lambda bb, c, r: (bb, r, c)),
        out_shape=jax.ShapeDtypeStruct((bsz, seq, ch), BF16),
        scratch_shapes=[pltpu.VMEM((CF_TT + 2 * CF_HALO, CF_CB), F32)],
        compiler_params=_cparams("arbitrary", "arbitrary", "arbitrary"),
        name="conformer_conv",
    )(g, g, g, w, b)


def _merge_kernel(cf_ref, ya_ref, sg_ref, x_ref, lng_ref, lnb_ref, wpb_ref, wout_ref, g1_ref,
                  nw_ref, sc_ref, sh_ref, o_ref, hx_ref):
    cf = cf_ref[...].astype(F32)
    mu = jnp.mean(cf, axis=-1, keepdims=True)
    xc = cf - mu
    var = jnp.mean(xc * xc, axis=-1, keepdims=True)
    ln = xc * lax.rsqrt(var + EPS) * lng_ref[...] + lnb_ref[...]
    yb = jnp.dot(_silu(ln).astype(BF16), wpb_ref[...], preferred_element_type=F32)
    m = (sg_ref[:, 0:D_MODEL].astype(F32) * ya_ref[...].astype(F32)
         + sg_ref[:, D_MODEL:2 * D_MODEL].astype(F32) * yb)
    mix = jnp.dot(m.astype(BF16), wout_ref[...], preferred_element_type=F32)
    x1 = x_ref[...] + g1_ref[0] * mix
    o_ref[...] = x1
    hx_ref[...] = (_rms(x1, nw_ref[...]) * (1.0 + sc_ref[0]) + sh_ref[0]).astype(hx_ref.dtype)


def _merge(cf, ya, sg, x2d, lng, lnb, wpb, wout, g1, nw, sc2, sh2, tokens_per_batch, tm=256):
    t = x2d.shape[0]
    tpb = tokens_per_batch // tm
    tok = lambda w: pl.BlockSpec((tm, w), lambda i: (i, 0))
    vec = pl.BlockSpec((1, D_MODEL), lambda i: (0, 0))
    mod = pl.BlockSpec((1, 1, D_MODEL), lambda i: (i // tpb, 0, 0))
    wspec = pl.BlockSpec((D_MODEL, D_MODEL), lambda i: (0, 0), pipeline_mode=pl.Buffered(1))
    return pl.pallas_call(
        _merge_kernel,
        grid=(t // tm,),
        in_specs=[tok(D_MODEL), tok(D_MODEL), tok(2 * D_MODEL), tok(D_MODEL), vec, vec, wspec, wspec,
                  mod, vec, mod, mod],
        out_specs=(tok(D_MODEL), tok(D_MODEL)),
        out_shape=(jax.ShapeDtypeStruct((t, D_MODEL), F32), jax.ShapeDtypeStruct((t, D_MODEL), BF16)),
        compiler_params=_cparams("arbitrary"),
        name="merge_out_proj",
    )(cf, ya, sg, x2d, lng, lnb, wpb, wout, g1, nw, sc2, sh2)


def _ffn_up_kernel(a_ref, wg_ref, wu_ref, o_ref):
    a = a_ref[...]
    gate = jnp.dot(a, wg_ref[...], preferred_element_type=F32)
    up = jnp.dot(a, wu_ref[...], preferred_element_type=F32)
    o_ref[...] = (_silu(gate) * up).astype(o_ref.dtype)


def _ffn_up(hx, wg, wu, tm=1024, tn=512):
    t, k = hx.shape
    n = wg.shape[1]
    return pl.pallas_call(
        _ffn_up_kernel,
        grid=(n // tn, t // tm),
        in_specs=[pl.BlockSpec((tm, k), lambda j, i: (i, 0)),
                  pl.BlockSpec((k, tn), lambda j, i: (0, j)),
                  pl.BlockSpec((k, tn), lambda j, i: (0, j))],
        out_specs=pl.BlockSpec((tm, tn), lambda j, i: (i, j)),
        out_shape=jax.ShapeDtypeStruct((t, n), BF16),
        compiler_params=_cparams("arbitrary", "arbitrary"),
        name="ffn_up",
    )(hx, wg, wu)


def _ffn_down_kernel(h_ref, w_ref, x_ref, g2_ref, nf_ref, o_ref):
    acc = jnp.dot(h_ref[...], w_ref[...], preferred_element_type=F32)
    o_ref[...] = _rms(x_ref[...] + g2_ref[0] * acc, nf_ref[...])


def _ffn_down(h1, wd, x1, g2, nf, tokens_per_batch, tm=256):
    t = x1.shape[0]
    tpb = tokens_per_batch // tm
    return pl.pallas_call(
        _ffn_down_kernel,
        grid=(t // tm,),
        in_specs=[pl.BlockSpec((tm, D_FF), lambda i: (i, 0)),
                  pl.BlockSpec((D_FF, D_MODEL), lambda i: (0, 0), pipeline_mode=pl.Buffered(1)),
                  pl.BlockSpec((tm, D_MODEL), lambda i: (i, 0)),
                  pl.BlockSpec((1, 1, D_MODEL), lambda i: (i // tpb, 0, 0)),
                  pl.BlockSpec((1, D_MODEL), lambda i: (0, 0))],
        out_specs=pl.BlockSpec((tm, D_MODEL), lambda i: (i, 0)),
        out_shape=jax.ShapeDtypeStruct((t, D_MODEL), F32),
        compiler_params=_cparams("arbitrary"),
        name="ffn_down",
    )(h1, wd, x1, g2, nf)


_HEAD_PERM = np.concatenate([np.arange(0, N_HEADS, 2), np.arange(1, N_HEADS, 2)])


def _expand_matrix(row_blocks):
    col_head = np.arange(D_INNER) // HEAD_DIM
    pat = (_HEAD_PERM[:, None] == col_head[None, :]).astype(np.float32)
    zero = np.zeros_like(pat)
    return jnp.asarray(np.concatenate([pat if on else zero for on in row_blocks], axis=0), BF16)


def _front(h3, bsz, seq, wxbc, wdt4, cw, cb, dtb4, alog4, r3, r2, h0f, h0b):
    xbc = _mm_xbc(h3, wxbc, cw, cb).reshape(bsz, seq, CONV_DIM)
    dt, da = _mm_dt(h3, wdt4, dtb4, alog4)
    dt = dt.reshape(bsz, seq, 2 * LANES)
    da = da.reshape(bsz, seq, 2 * LANES)
    yf, yb, hf, hb = _ssd(xbc, dt, da, r3, r2, h0f, h0b)
    return xbc, yf, yb, hf, hb


@jax.jit
def kernel(x, c, ctx, c_ctx, w_mod, b_mod, norm_mix, w_in, ssm_conv_w, ssm_conv_b, dt_bias, a_log,
           d_skip, ssm_norm, cf_conv_w, cf_conv_b, cf_ln_g, cf_ln_b, w_proj_a, w_proj_b, w_out,
           norm_ffn, w_ffn_gate, w_ffn_up, w_ffn_down, norm_final):
    bsz, seq, _ = x.shape
    ctx_len = ctx.shape[1]
    t = bsz * seq

    cc = jnp.zeros((8, D_MODEL), F32).at[0:bsz].set(c).at[bsz].set(c_ctx)
    mod = _ada(cc, w_mod[0], b_mod[0])
    sh1, sc1, g1, sh2, sc2, g2 = [m.reshape(8, 1, D_MODEL) for m in jnp.split(mod, 6, axis=-1)]
    lat = lambda m: m[0:bsz]
    cx = lambda m: jnp.broadcast_to(m[bsz:bsz + 1], (bsz, 1, D_MODEL))

    w = w_in[0]
    wz = w[:, 0:OFF_Z].astype(BF16)
    wxbc = w[:, OFF_Z:OFF_XBC].astype(BF16)
    wdt = w[:, OFF_XBC:OFF_DT]
    wdt_f = wdt[:, 0:N_HEADS][:, _HEAD_PERM]
    wdt_b = wdt[:, N_HEADS:2 * N_HEADS][:, _HEAD_PERM]
    wdt4 = jnp.concatenate([wdt_f, wdt_b, wdt_b, wdt_f], axis=1).astype(BF16)
    wglu_u = w[:, OFF_DT:OFF_DT + D_MODEL].astype(BF16)
    wglu_v = w[:, OFF_DT + D_MODEL:OFF_GLU].astype(BF16)
    wgates = w[:, OFF_GLU:].astype(BF16)
    perm2 = lambda p: jnp.concatenate([p[0][_HEAD_PERM], p[1][_HEAD_PERM],
                                       p[1][_HEAD_PERM], p[0][_HEAD_PERM]]).reshape(1, 2 * LANES)
    dtb4 = perm2(dt_bias[0])
    alog4 = perm2(a_log[0])
    cw = jnp.zeros((8, CONV_DIM), F32).at[0:SSM_CONV].set(ssm_conv_w[0])
    cb = ssm_conv_b[0].reshape(1, CONV_DIM)
    r3 = _expand_matrix([True, True, True, False])
    r2 = _expand_matrix([True, False, True, False])
    dsk = jnp.repeat(d_skip[0], HEAD_DIM).reshape(1, D_INNER)
    cfw = jnp.zeros((32, D_MODEL), F32).at[0:CF_KERNEL].set(cf_conv_w[0])

    hc = _normmod(ctx, norm_mix[0], cx(sc1), cx(sh1), tm=ctx_len)
    zero_state = jnp.zeros((bsz, D_STATE, D_INNER), F32)
    _, _, _, hf, hb = _front(hc, bsz, ctx_len, wxbc, wdt4, cw, cb, dtb4, alog4, r3, r2,
                             zero_state, zero_state)

    x2d = x.reshape(t, D_MODEL)
    hx = _normmod(x, norm_mix[0], lat(sc1), lat(sh1), tm=1024)
    xbc, yf, yb, _, _ = _front(hx, bsz, seq, wxbc, wdt4, cw, cb, dtb4, alog4, r3, r2, hf, hb)
    sz, sg = _mm_zg(hx, wz, wgates)
    glu = _mm_glu(hx, wglu_u, wglu_v)

    ya = _gate_proja(yf.reshape(t, D_INNER), yb.reshape(t, D_INNER), xbc.reshape(t, CONV_DIM), sz,
                     dsk, ssm_norm[0].reshape(1, D_INNER), w_proj_a[0].astype(BF16))
    cf = _cfconv(glu.reshape(bsz, seq, D_MODEL), cfw, cf_conv_b[0].reshape(1, D_MODEL))
    x1, hx2 = _merge(cf.reshape(t, D_MODEL), ya, sg, x2d, cf_ln_g[0].reshape(1, D_MODEL),
                     cf_ln_b[0].reshape(1, D_MODEL), w_proj_b[0].astype(BF16), w_out[0].astype(BF16),
                     lat(g1), norm_ffn[0].reshape(1, D_MODEL), lat(sc2), lat(sh2), seq)
    h1 = _ffn_up(hx2, w_ffn_gate[0].astype(BF16), w_ffn_up[0].astype(BF16))
    out = _ffn_down(h1, w_ffn_down[0].astype(BF16), x1, lat(g2), norm_final.reshape(1, D_MODEL), seq)
    return out.reshape(bsz, seq, D_MODEL)
```

```python
import jax
import jax.numpy as jnp
import numpy as np
from jax import lax
from jax.experimental import pallas as pl
from jax.experimental.pallas import tpu as pltpu

F32 = jnp.float32
BF16 = jnp.bfloat16

D_MODEL = 2048
GRID_W = 64
D_INNER = 2 * D_MODEL
HEAD_DIM = 64
N_HEADS = D_INNER // HEAD_DIM
N_GROUPS = 8
GROUP_W = D_INNER // N_GROUPS
D_STATE = 128
SSM_CONV = 5
CHUNK = 128
BC_DIM = N_GROUPS * D_STATE
CONV_DIM = D_INNER + 2 * BC_DIM
CF_KERNEL = 31
D_FF = 5632
EPS = 1e-6
OFF_Z = D_INNER
OFF_XBC = OFF_Z + CONV_DIM
OFF_DT = OFF_XBC + 2 * N_HEADS
OFF_GLU = OFF_DT + 2 * D_MODEL

LANES = 128
HALF = LANES // 2
MXU_N = 256
SUB_N = 2 * MXU_N
VMEM_LIMIT = 56 * 1024 * 1024
NEG_BIG = -1e30
HALO = 16
LOG2E = 1.4426950408889634


def _cparams(*sem):
    return pltpu.CompilerParams(dimension_semantics=sem, vmem_limit_bytes=VMEM_LIMIT)


def _sigmoid(v):
    return jax.nn.sigmoid(v)


def _silu(v):
    return v * jax.nn.sigmoid(v)


def _rms(v, w):
    return v * lax.rsqrt(jnp.mean(v * v, axis=-1, keepdims=True) + EPS) * w


def _cast_weight_once(w_ref, wbf_ref):
    @pl.when(pl.program_id(1) == 0)
    def _():
        wbf_ref[...] = w_ref[...].astype(BF16)


def _ada_kernel(c_ref, w_ref, b_ref, o_ref):
    s = _silu(c_ref[...]).astype(BF16)
    o_ref[...] = jnp.dot(s, w_ref[...].astype(BF16), preferred_element_type=F32) + b_ref[...]


def _ada(cc, w_mod, b_mod):
    n = w_mod.shape[1]
    tn = 1024
    return pl.pallas_call(
        _ada_kernel,
        grid=(n // tn,),
        in_specs=[pl.BlockSpec((8, D_MODEL), lambda j: (0, 0)),
                  pl.BlockSpec((D_MODEL, tn), lambda j: (0, j)),
                  pl.BlockSpec((1, tn), lambda j: (0, j))],
        out_specs=pl.BlockSpec((8, tn), lambda j: (0, j)),
        out_shape=jax.ShapeDtypeStruct((8, n), F32),
        compiler_params=_cparams("arbitrary"),
        name="ada_params",
    )(cc, w_mod, b_mod.reshape(1, n))


def _normmod_kernel(xp_ref, x_ref, xn_ref, w_ref, sc_ref, sh_ref, o_ref):
    i = pl.program_id(1)
    n = pl.num_programs(1)
    tm = x_ref.shape[1]

    def nm(v):
        return _rms(v, w_ref[...]) * (1.0 + sc_ref[0]) + sh_ref[0]

    o_ref[0, 0:tm, :] = nm(x_ref[0]).astype(o_ref.dtype)
    nxt = jnp.where(i < n - 1, nm(xn_ref[0]), 0.0)
    prv = jnp.where(i > 0, nm(xp_ref[0]), 0.0)
    o_ref[0, tm:tm + HALO, :] = jnp.concatenate([nxt, prv], axis=0).astype(o_ref.dtype)


def _normmod(x3d, w, sc, sh, tm):
    bsz, seq, _ = x3d.shape
    nt = seq // tm
    r8 = tm // 8
    n8 = seq // 8
    mod = pl.BlockSpec((1, 1, D_MODEL), lambda b, i: (b, 0, 0))
    return pl.pallas_call(
        _normmod_kernel,
        grid=(bsz, nt),
        in_specs=[pl.BlockSpec((1, 8, D_MODEL), lambda b, i: (b, jnp.maximum(i * r8 - 1, 0), 0)),
                  pl.BlockSpec((1, tm, D_MODEL), lambda b, i: (b, i, 0)),
                  pl.BlockSpec((1, 8, D_MODEL), lambda b, i: (b, jnp.minimum((i + 1) * r8, n8 - 1), 0)),
                  pl.BlockSpec((1, D_MODEL), lambda b, i: (0, 0)), mod, mod],
        out_specs=pl.BlockSpec((1, tm + HALO, D_MODEL), lambda b, i: (b * nt + i, 0, 0)),
        out_shape=jax.ShapeDtypeStruct((bsz * nt, tm + HALO, D_MODEL), BF16),
        compiler_params=_cparams("arbitrary", "arbitrary"),
        name="norm_modulate",
    )(x3d, x3d, x3d, w.reshape(1, D_MODEL), sc, sh)


def _glu_kernel(a_ref, wu_ref, wv_ref, o_ref):
    a = a_ref[0]
    for nb in range(o_ref.shape[1] // SUB_N):
        cs = slice(nb * SUB_N, (nb + 1) * SUB_N)
        sv = _sigmoid(jnp.dot(a, wv_ref[:, cs], preferred_element_type=F32))
        u = jnp.dot(a, wu_ref[:, cs], preferred_element_type=F32)
        o_ref[:, cs] = (u * sv).astype(o_ref.dtype)


def _mm_glu(a3, wu, wv, tn=1024):
    ntile, tme, k = a3.shape
    tm = tme - HALO
    n = wu.shape[1]
    return pl.pallas_call(
        _glu_kernel,
        grid=(n // tn, ntile),
        in_specs=[pl.BlockSpec((1, tm, k), lambda j, i: (i, 0, 0)),
                  pl.BlockSpec((k, tn), lambda j, i: (0, j)),
                  pl.BlockSpec((k, tn), lambda j, i: (0, j))],
        out_specs=pl.BlockSpec((tm, tn), lambda j, i: (i, j)),
        out_shape=jax.ShapeDtypeStruct((ntile * tm, n), BF16),
        compiler_params=_cparams("arbitrary", "arbitrary"),
        name="proj_glu",
    )(a3, wu, wv)


def _zg_kernel(a_ref, wz_ref, wg_ref, oz_ref, og_ref, wzbf_ref):
    _cast_weight_once(wz_ref, wzbf_ref)
    a = a_ref[0]
    for nb in range(oz_ref.shape[1] // SUB_N):
        cs = slice(nb * SUB_N, (nb + 1) * SUB_N)
        oz_ref[:, cs] = _silu(jnp.dot(a, wzbf_ref[:, cs], preferred_element_type=F32)).astype(oz_ref.dtype)
        og_ref[:, cs] = _sigmoid(jnp.dot(a, wg_ref[:, cs], preferred_element_type=F32)).astype(og_ref.dtype)


def _mm_zg(a3, w_in, wg, tn=1024):
    ntile, tme, k = a3.shape
    tm = tme - HALO
    n = wg.shape[1]
    wspec = pl.BlockSpec((k, tn), lambda j, i: (0, j))
    ospec = pl.BlockSpec((tm, tn), lambda j, i: (i, j))
    shape = jax.ShapeDtypeStruct((ntile * tm, n), BF16)
    return pl.pallas_call(
        _zg_kernel,
        grid=(n // tn, ntile),
        in_specs=[pl.BlockSpec((1, tm, k), lambda j, i: (i, 0, 0)), wspec, wspec],
        out_specs=(ospec, ospec),
        out_shape=(shape, shape),
        scratch_shapes=[pltpu.VMEM((k, tn), BF16)],
        compiler_params=_cparams("arbitrary", "arbitrary"),
        name="proj_z_gates",
    )(a3, w_in, wg)


def _dt_kernel(a_ref, w_ref, bias_ref, alog_ref, dt_ref, da_ref):
    raw = jnp.dot(a_ref[0], w_ref[...], preferred_element_type=F32) + bias_ref[...]
    dt = jnp.maximum(raw, 0.0) + jnp.log1p(jnp.exp(-jnp.abs(raw)))
    dt_ref[...] = dt
    da_ref[...] = dt * (-jnp.exp(alog_ref[...])) * LOG2E


def _mm_dt(a3, w, bias, alog):
    ntile, tme, k = a3.shape
    tm = tme - HALO
    n = w.shape[1]
    vec = pl.BlockSpec((1, n), lambda i: (0, 0))
    out = pl.BlockSpec((tm, n), lambda i: (i, 0))
    shape = jax.ShapeDtypeStruct((ntile * tm, n), F32)
    return pl.pallas_call(
        _dt_kernel,
        grid=(ntile,),
        in_specs=[pl.BlockSpec((1, tm, k), lambda i: (i, 0, 0)),
                  pl.BlockSpec((k, n), lambda i: (0, 0)), vec, vec],
        out_specs=(out, out),
        out_shape=(shape, shape),
        compiler_params=_cparams("arbitrary"),
        name="proj_dt",
    )(a3, w, bias, alog)


def _xbc_kernel(a_ref, w_ref, cw_ref, cb_ref, o_ref, wbf_ref):
    _cast_weight_once(w_ref, wbf_ref)
    a = a_ref[0]
    n = a.shape[0]
    tm = n - HALO
    for nb in range(o_ref.shape[1] // MXU_N):
        cs = slice(nb * MXU_N, (nb + 1) * MXU_N)
        p = jnp.dot(a, wbf_ref[:, cs], preferred_element_type=F32)
        acc = cb_ref[:, cs] + cw_ref[2:3, cs] * p
        for k in (0, 1, 3, 4):
            acc = acc + cw_ref[k:k + 1, cs] * pltpu.roll(p, (2 - k) % n, 0)
        o_ref[:, cs] = _silu(acc[0:tm]).astype(o_ref.dtype)


def _mm_xbc(a3, w_in, cw, cb, tn=1024):
    ntile, tme, k = a3.shape
    tm = tme - HALO
    n = CONV_DIM
    col0 = OFF_Z // tn
    return pl.pallas_call(
        _xbc_kernel,
        grid=(n // tn, ntile),
        in_specs=[pl.BlockSpec((1, tme, k), lambda j, i: (i, 0, 0)),
                  pl.BlockSpec((k, tn), lambda j, i: (0, col0 + j)),
                  pl.BlockSpec((8, tn), lambda j, i: (0, j)),
                  pl.BlockSpec((1, tn), lambda j, i: (0, j))],
        out_specs=pl.BlockSpec((tm, tn), lambda j, i: (i, j)),
        out_shape=jax.ShapeDtypeStruct((ntile * tm, n), BF16),
        scratch_shapes=[pltpu.VMEM((k, tn), BF16)],
        compiler_params=_cparams("arbitrary", "arbitrary"),
        name="proj_xbc_conv",
    )(a3, w_in, cw, cb)


def _ssd_prologue(xs_ref, b_ref, c_ref, dt_ref, da_ref, r3_ref, r2_ref,
                  ecs_ref, xdt_ref, xw_ref, row_ref, cb_ref, reverse):
    q = CHUNK
    da = da_ref[0]
    ri = lax.broadcasted_iota(jnp.int32, (q, q), 0)
    ci = lax.broadcasted_iota(jnp.int32, (q, q), 1)
    tri = (ri <= ci) if reverse else (ri >= ci)
    cs = jnp.dot(tri.astype(F32), da, precision=lax.Precision.HIGHEST,
                 preferred_element_type=F32)
    cst = cs.T
    lane_h = lax.broadcasted_iota(jnp.int32, (N_HEADS // 2, LANES), 1) < HALF
    ev = cst[0:N_HEADS // 2]
    od = cst[N_HEADS // 2:N_HEADS]
    row_ref[0] = jnp.where(lane_h, ev, pltpu.roll(od, HALF, 1))
    row_ref[1] = jnp.where(lane_h, pltpu.roll(ev, HALF, 1), od)

    lane_q = lax.broadcasted_iota(jnp.int32, (q, LANES), 1)
    lo_half = lane_q < HALF
    hi = cs.astype(BF16).astype(F32)
    r1 = cs - hi
    mid = r1.astype(BF16).astype(F32)
    lo = (r1 - mid).astype(BF16).astype(F32)
    x1 = jnp.where(lo_half, hi, pltpu.roll(mid, HALF, 1))
    x2 = jnp.where(lo_half, lo, 0.0)
    cparts = jnp.concatenate([x1, x2], axis=1).astype(BF16)
    dt = dt_ref[0]
    dhi = dt.astype(BF16)
    dmid = (dt - dhi.astype(F32)).astype(BF16)
    dparts = jnp.concatenate([dhi, dmid], axis=1)

    last = 0 if reverse else q - 1
    for g in range(N_GROUPS):
        gs = slice(g * GROUP_W, (g + 1) * GROUP_W)
        ecs = jnp.dot(cparts, r3_ref[:, gs], preferred_element_type=F32)
        ecs_ref[:, gs] = ecs
        edt = jnp.dot(dparts, r2_ref[:, gs], preferred_element_type=F32)
        xdt = xs_ref[0, :, gs].astype(F32) * edt
        xdt_ref[:, gs] = xdt.astype(BF16)
        xw_ref[:, gs] = (xdt * jnp.exp2(ecs[last:last + 1, :] - ecs)).astype(BF16)

    for g in range(N_GROUPS):
        cg = c_ref[0, :, g * D_STATE:(g + 1) * D_STATE]
        bg = b_ref[0, :, g * D_STATE:(g + 1) * D_STATE]
        cb = lax.dot_general(cg, bg, (((1,), (1,)), ((), ())), preferred_element_type=F32)
        cbr = pltpu.roll(cb, HALF, 1)
        cb_ref[g, 0] = jnp.where(lo_half, cb, cbr)
        cb_ref[g, 1] = jnp.where(lo_half, cbr, cb)


def _ssd_main(b_ref, c_ref, y_ref, h_ref, ecs_ref, xdt_ref, xw_ref, row_ref, cb_ref, reverse):
    q = CHUNK
    last = 0 if reverse else q - 1
    lane_q = lax.broadcasted_iota(jnp.int32, (q, LANES), 1)
    lo_half = lane_q < HALF
    pos = lax.broadcasted_iota(jnp.int32, (q, LANES), 0)
    s0 = jnp.where(lo_half, lane_q, lane_q - HALF)
    s1 = s0 + HALF
    m0 = (pos <= s0) if reverse else (pos >= s0)
    m1 = (pos <= s1) if reverse else (pos >= s1)
    lo64 = lax.broadcasted_iota(jnp.int32, (HALF, LANES), 1) < HALF
    zero = jnp.zeros((HALF, LANES), BF16)

    for g in range(N_GROUPS):
        gs = slice(g * GROUP_W, (g + 1) * GROUP_W)
        cg = c_ref[0, :, g * D_STATE:(g + 1) * D_STATE]
        bg = b_ref[0, :, g * D_STATE:(g + 1) * D_STATE]
        cb0 = cb_ref[g, 0]
        cb1 = cb_ref[g, 1]
        hg = h_ref[0, :, gs]
        yoff = jnp.dot(cg, hg.astype(BF16), preferred_element_type=F32)
        for p in range(GROUP_W // LANES):
            k = g * (GROUP_W // LANES) + p
            sl = slice(k * LANES, (k + 1) * LANES)
            a = ecs_ref[:, sl]
            t0 = jnp.exp2(jnp.where(m0, a - row_ref[0, k:k + 1, :], NEG_BIG)) * cb0
            t1 = jnp.exp2(jnp.where(m1, a - row_ref[1, k:k + 1, :], NEG_BIG)) * cb1
            lhs = jnp.concatenate([t0, t1], axis=1).astype(BF16)
            xa = xdt_ref[0:HALF, sl]
            xb = xdt_ref[HALF:q, sl]
            rhs = jnp.concatenate([jnp.where(lo64, xa, zero), jnp.where(lo64, zero, xa),
                                   jnp.where(lo64, xb, zero), jnp.where(lo64, zero, xb)], axis=0)
            yd = jnp.dot(lhs, rhs, preferred_element_type=F32)
            y = yd + yoff[:, p * LANES:(p + 1) * LANES] * jnp.exp2(a)
            y_ref[0, :, sl] = y.astype(y_ref.dtype)
        st = lax.dot_general(bg, xw_ref[:, gs], (((0,), (0,)), ((), ())), preferred_element_type=F32)
        h_ref[0, :, gs] = hg * jnp.exp2(ecs_ref[last:last + 1, gs]) + st


def _ssd_kernel(xsf_ref, bf_ref, cf_ref, dtf_ref, daf_ref, xsb_ref, bb_ref, cbk_ref, dtb_ref, dab_ref,
                r3_ref, r2_ref, h0f_ref, h0b_ref, yf_ref, yb_ref, hf_ref, hb_ref,
                ecs_ref, xdt_ref, xw_ref, row_ref, cb_ref):
    @pl.when(pl.program_id(1) == 0)
    def _():
        hf_ref[...] = h0f_ref[...]
        hb_ref[...] = h0b_ref[...]

    sc = lambda d: (ecs_ref.at[d], xdt_ref.at[d], xw_ref.at[d], row_ref.at[d], cb_ref.at[d])
    _ssd_prologue(xsf_ref, bf_ref, cf_ref, dtf_ref, daf_ref, r3_ref, r2_ref, *sc(0), reverse=False)
    _ssd_prologue(xsb_ref, bb_ref, cbk_ref, dtb_ref, dab_ref, r3_ref, r2_ref, *sc(1), reverse=True)
    _ssd_main(bf_ref, cf_ref, yf_ref, hf_ref, *sc(0), reverse=False)
    _ssd_main(bb_ref, cbk_ref, yb_ref, hb_ref, *sc(1), reverse=True)


def _ssd(xbc, dt, da, r3, r2, h0f, h0b):
    bsz, seq, _ = xbc.shape
    nc = seq // CHUNK
    q = CHUNK
    bcol = D_INNER // BC_DIM
    fwd = lambda c: c
    bwd = lambda c: nc - 1 - c
    const = lambda r, c: pl.BlockSpec((r, c), lambda b, cc: (0, 0), pipeline_mode=pl.Buffered(1))
    state = pl.BlockSpec((1, D_STATE, D_INNER), lambda b, c: (b, 0, 0))

    def chunk_specs(ci, dcol):
        return [pl.BlockSpec((1, q, D_INNER), lambda b, c: (b, ci(c), 0)),
                pl.BlockSpec((1, q, BC_DIM), lambda b, c: (b, ci(c), bcol)),
                pl.BlockSpec((1, q, BC_DIM), lambda b, c: (b, ci(c), bcol + 1)),
                pl.BlockSpec((1, q, LANES), lambda b, c: (b, ci(c), dcol)),
                pl.BlockSpec((1, q, LANES), lambda b, c: (b, ci(c), dcol))]

    return pl.pallas_call(
        _ssd_kernel,
        grid=(bsz, nc),
        in_specs=chunk_specs(fwd, 0) + chunk_specs(bwd, 1)
        + [const(2 * LANES, D_INNER), const(2 * LANES, D_INNER), state, state],
        out_specs=(pl.BlockSpec((1, q, D_INNER), lambda b, c: (b, fwd(c), 0)),
                   pl.BlockSpec((1, q, D_INNER), lambda b, c: (b, bwd(c), 0)),
                   state, state),
        out_shape=(jax.ShapeDtypeStruct((bsz, seq, D_INNER), BF16),
                   jax.ShapeDtypeStruct((bsz, seq, D_INNER), BF16),
                   jax.ShapeDtypeStruct((bsz, D_STATE, D_INNER), F32),
                   jax.ShapeDtypeStruct((bsz, D_STATE, D_INNER), F32)),
        scratch_shapes=[pltpu.VMEM((2, q, D_INNER), F32),
                        pltpu.VMEM((2, q, D_INNER), BF16),
                        pltpu.VMEM((2, q, D_INNER), BF16),
                        pltpu.VMEM((2, 2, N_HEADS // 2, LANES), F32),
                        pltpu.VMEM((2, N_GROUPS, 2, q, LANES), F32)],
        compiler_params=_cparams("arbitrary", "arbitrary"),
        name="ssd_scan",
    )(xbc, xbc, xbc, dt, da, xbc, xbc, xbc, dt, da, r3, r2, h0f, h0b)


def _gate_proja_kernel(yf_ref, yb_ref, xs_ref, sz_ref, dsk_ref, nw_ref, w_ref, o_ref):
    y = (yf_ref[...].astype(F32) + yb_ref[...].astype(F32)
         + dsk_ref[...] * xs_ref[...].astype(F32))
    u = y * sz_ref[...].astype(F32)
    un = _rms(u, nw_ref[...]).astype(BF16)
    o_ref[...] = jnp.dot(un, w_ref[...], preferred_element_type=F32).astype(o_ref.dtype)


def _gate_proja(yf, yb, xbc, sz, dsk, nw, w, tm=256):
    t = yf.shape[0]
    tok = pl.BlockSpec((tm, D_INNER), lambda i: (i, 0))
    vec = pl.BlockSpec((1, D_INNER), lambda i: (0, 0))
    return pl.pallas_call(
        _gate_proja_kernel,
        grid=(t // tm,),
        in_specs=[tok, tok, tok, tok, vec, vec,
                  pl.BlockSpec((D_INNER, D_MODEL), lambda i: (0, 0), pipeline_mode=pl.Buffered(1))],
        out_specs=pl.BlockSpec((tm, D_MODEL), lambda i: (i, 0)),
        out_shape=jax.ShapeDtypeStruct((t, D_MODEL), BF16),
        compiler_params=_cparams("arbitrary"),
        name="gate_proj_a",
    )(yf, yb, xbc, sz, dsk, nw, w)


CF_TT = 2048
CF_HALO = 1024
CF_CB = 256


def _cfconv_kernel(gp_ref, gc_ref, gn_ref, w_ref, b_ref, o_ref, win_ref):
    r = pl.program_id(2)
    n = pl.num_programs(2)
    win_ref[0:CF_HALO, :] = jnp.where(r > 0, gp_ref[0].astype(F32), 0.0)
    win_ref[CF_HALO:CF_HALO + CF_TT, :] = gc_ref[0].astype(F32)
    win_ref[CF_HALO + CF_TT:2 * CF_HALO + CF_TT, :] = jnp.where(r < n - 1, gn_ref[0].astype(F32), 0.0)
    half = CF_KERNEL // 2

    def body(rr, carry):
        base = pl.multiple_of(CF_HALO + 2 * rr * GRID_W - half * GRID_W, GRID_W)
        acc0 = jnp.zeros((GRID_W, CF_CB), F32) + b_ref[...]
        acc1 = acc0
        for j in range(CF_KERNEL + 1):
            xrow = win_ref[pl.ds(base + j * GRID_W, GRID_W), :]
            if j < CF_KERNEL:
                acc0 = acc0 + w_ref[j:j + 1, :] * xrow
            if j > 0:
                acc1 = acc1 + w_ref[j - 1:j, :] * xrow
        out0 = pl.multiple_of(2 * rr * GRID_W, GRID_W)
        o_ref[0, pl.ds(out0, GRID_W), :] = acc0.astype(o_ref.dtype)
        o_ref[0, pl.ds(pl.multiple_of(out0 + GRID_W, GRID_W), GRID_W), :] = acc1.astype(o_ref.dtype)
        return carry

    lax.fori_loop(0, CF_TT // (2 * GRID_W), body, 0)


def _cfconv(g, w, b):
    bsz, seq, ch = g.shape
    nt = seq // CF_TT
    hpt = CF_TT // CF_HALO
    nh = seq // CF_HALO
    blk = lambda f: pl.BlockSpec((1, CF_TT, CF_CB), f)
    hblk = lambda f: pl.BlockSpec((1, CF_HALO, CF_CB), f)
    return pl.pallas_call(
        _cfconv_kernel,
        grid=(bsz, ch // CF_CB, nt),
        in_specs=[hblk(lambda bb, c, r: (bb, jnp.maximum(r * hpt - 1, 0), c)),
                  blk(lambda bb, c, r: (bb, r, c)),
                  hblk(lambda bb, c, r: (bb, jnp.minimum((r + 1) * hpt, nh - 1), c)),
                  pl.BlockSpec((32, CF_CB), lambda bb, c, r: (0, c)),
                  pl.BlockSpec((1, CF_CB), lambda bb, c, r: (0, c))],
        out_specs=blk(lambda bb, c, r: (bb, r, c)),
        out_shape=jax.ShapeDtypeStruct((bsz, seq, ch), BF16),
        scratch_shapes=[pltpu.VMEM((CF_TT + 2 * CF_HALO, CF_CB), F32)],
        compiler_params=_cparams("arbitrary", "arbitrary", "arbitrary"),
        name="conformer_conv",
    )(g, g, g, w, b)


def _merge_kernel(cf_ref, ya_ref, sg_ref, x_ref, lng_ref, lnb_ref, wpb_ref, wout_ref, g1_ref,
                  nw_ref, sc_ref, sh_ref, o_ref, hx_ref):
    cf = cf_ref[...].astype(F32)
    mu = jnp.mean(cf, axis=-1, keepdims=True)
    xc = cf - mu
    var = jnp.mean(xc * xc, axis=-1, keepdims=True)
    ln = xc * lax.rsqrt(var + EPS) * lng_ref[...] + lnb_ref[...]
    yb = jnp.dot(_silu(ln).astype(BF16), wpb_ref[...], preferred_element_type=F32)
    m = (sg_ref[:, 0:D_MODEL].astype(F32) * ya_ref[...].astype(F32)
         + sg_ref[:, D_MODEL:2 * D_MODEL].astype(F32) * yb)
    mix = jnp.dot(m.astype(BF16), wout_ref[...], preferred_element_type=F32)
    x1 = x_ref[...] + g1_ref[0] * mix
    o_ref[...] = x1
    hx_ref[...] = (_rms(x1, nw_ref[...]) * (1.0 + sc_ref[0]) + sh_ref[0]).astype(hx_ref.dtype)


def _merge(cf, ya, sg, x2d, lng, lnb, wpb, wout, g1, nw, sc2, sh2, tokens_per_batch, tm=256):
    t = x2d.shape[0]
    tpb = tokens_per_batch // tm
    tok = lambda w: pl.BlockSpec((tm, w), lambda i: (i, 0))
    vec = pl.BlockSpec((1, D_MODEL), lambda i: (0, 0))
    mod = pl.BlockSpec((1, 1, D_MODEL), lambda i: (i // tpb, 0, 0))
    wspec = pl.BlockSpec((D_MODEL, D_MODEL), lambda i: (0, 0), pipeline_mode=pl.Buffered(1))
    return pl.pallas_call(
        _merge_kernel,
        grid=(t // tm,),
        in_specs=[tok(D_MODEL), tok(D_MODEL), tok(2 * D_MODEL), tok(D_MODEL), vec, vec, wspec, wspec,
                  mod, vec, mod, mod],
        out_specs=(tok(D_MODEL), tok(D_MODEL)),
        out_shape=(jax.ShapeDtypeStruct((t, D_MODEL), F32), jax.ShapeDtypeStruct((t, D_MODEL), BF16)),
        compiler_params=_cparams("arbitrary"),
        name="merge_out_proj",
    )(cf, ya, sg, x2d, lng, lnb, wpb, wout, g1, nw, sc2, sh2)


def _ffn_up_kernel(a_ref, wg_ref, wu_ref, o_ref, wgbf_ref, wubf_ref):
    _cast_weight_once(wg_ref, wgbf_ref)
    _cast_weight_once(wu_ref, wubf_ref)
    a = a_ref[...]
    gate = jnp.dot(a, wgbf_ref[...], preferred_element_type=F32)
    up = jnp.dot(a, wubf_ref[...], preferred_element_type=F32)
    o_ref[...] = (_silu(gate) * up).astype(o_ref.dtype)


def _ffn_up(hx, wg, wu, tm=1024, tn=512):
    t, k = hx.shape
    n = wg.shape[1]
    return pl.pallas_call(
        _ffn_up_kernel,
        grid=(n // tn, t // tm),
        in_specs=[pl.BlockSpec((tm, k), lambda j, i: (i, 0)),
                  pl.BlockSpec((k, tn), lambda j, i: (0, j)),
                  pl.BlockSpec((k, tn), lambda j, i: (0, j))],
        out_specs=pl.BlockSpec((tm, tn), lambda j, i: (i, j)),
        out_shape=jax.ShapeDtypeStruct((t, n), BF16),
        scratch_shapes=[pltpu.VMEM((k, tn), BF16), pltpu.VMEM((k, tn), BF16)],
        compiler_params=_cparams("arbitrary", "arbitrary"),
        name="ffn_up",
    )(hx, wg, wu)


def _ffn_down_kernel(h_ref, w_ref, x_ref, g2_ref, nf_ref, o_ref):
    acc = jnp.dot(h_ref[...], w_ref[...], preferred_element_type=F32)
    o_ref[...] = _rms(x_ref[...] + g2_ref[0] * acc, nf_ref[...])


def _ffn_down(h1, wd, x1, g2, nf, tokens_per_batch, tm=256):
    t = x1.shape[0]
    tpb = tokens_per_batch // tm
    return pl.pallas_call(
        _ffn_down_kernel,
        grid=(t // tm,),
        in_specs=[pl.BlockSpec((tm, D_FF), lambda i: (i, 0)),
                  pl.BlockSpec((D_FF, D_MODEL), lambda i: (0, 0), pipeline_mode=pl.Buffered(1)),
                  pl.BlockSpec((tm, D_MODEL), lambda i: (i, 0)),
                  pl.BlockSpec((1, 1, D_MODEL), lambda i: (i // tpb, 0, 0)),
                  pl.BlockSpec((1, D_MODEL), lambda i: (0, 0))],
        out_specs=pl.BlockSpec((tm, D_MODEL), lambda i: (i, 0)),
        out_shape=jax.ShapeDtypeStruct((t, D_MODEL), F32),
        compiler_params=_cparams("arbitrary"),
        name="ffn_down",
    )(h1, wd, x1, g2, nf)


_HEAD_PERM = np.concatenate([np.arange(0, N_HEADS, 2), np.arange(1, N_HEADS, 2)])


def _expand_matrix(row_blocks):
    col_head = np.arange(D_INNER) // HEAD_DIM
    pat = (_HEAD_PERM[:, None] == col_head[None, :]).astype(np.float32)
    zero = np.zeros_like(pat)
    return jnp.asarray(np.concatenate([pat if on else zero for on in row_blocks], axis=0), BF16)


def _front(h3, bsz, seq, w_in, wdt4, cw, cb, dtb4, alog4, r3, r2, h0f, h0b):
    xbc = _mm_xbc(h3, w_in, cw, cb).reshape(bsz, seq, CONV_DIM)
    dt, da = _mm_dt(h3, wdt4, dtb4, alog4)
    dt = dt.reshape(bsz, seq, 2 * LANES)
    da = da.reshape(bsz, seq, 2 * LANES)
    yf, yb, hf, hb = _ssd(xbc, dt, da, r3, r2, h0f, h0b)
    return xbc, yf, yb, hf, hb


@jax.jit
def kernel(x, c, ctx, c_ctx, w_mod, b_mod, norm_mix, w_in, ssm_conv_w, ssm_conv_b, dt_bias, a_log,
           d_skip, ssm_norm, cf_conv_w, cf_conv_b, cf_ln_g, cf_ln_b, w_proj_a, w_proj_b, w_out,
           norm_ffn, w_ffn_gate, w_ffn_up, w_ffn_down, norm_final):
    bsz, seq, _ = x.shape
    ctx_len = ctx.shape[1]
    t = bsz * seq

    cc = jnp.zeros((8, D_MODEL), F32).at[0:bsz].set(c).at[bsz].set(c_ctx)
    mod = _ada(cc, w_mod[0], b_mod[0])
    sh1, sc1, g1, sh2, sc2, g2 = [m.reshape(8, 1, D_MODEL) for m in jnp.split(mod, 6, axis=-1)]
    lat = lambda m: m[0:bsz]
    cx = lambda m: jnp.broadcast_to(m[bsz:bsz + 1], (bsz, 1, D_MODEL))

    w = w_in[0]
    wdt = w[:, OFF_XBC:OFF_DT]
    wdt_f = wdt[:, 0:N_HEADS][:, _HEAD_PERM]
    wdt_b = wdt[:, N_HEADS:2 * N_HEADS][:, _HEAD_PERM]
    wdt4 = jnp.concatenate([wdt_f, wdt_b, wdt_b, wdt_f], axis=1).astype(BF16)
    wglu_u = w[:, OFF_DT:OFF_DT + D_MODEL].astype(BF16)
    wglu_v = w[:, OFF_DT + D_MODEL:OFF_GLU].astype(BF16)
    wgates = w[:, OFF_GLU:].astype(BF16)
    perm2 = lambda p: jnp.concatenate([p[0][_HEAD_PERM], p[1][_HEAD_PERM],
                                       p[1][_HEAD_PERM], p[0][_HEAD_PERM]]).reshape(1, 2 * LANES)
    dtb4 = perm2(dt_bias[0])
    alog4 = perm2(a_log[0])
    cw = jnp.zeros((8, CONV_DIM), F32).at[0:SSM_CONV].set(ssm_conv_w[0])
    cb = ssm_conv_b[0].reshape(1, CONV_DIM)
    r3 = _expand_matrix([True, True, True, False])
    r2 = _expand_matrix([True, False, True, False])
    dsk = jnp.repeat(d_skip[0], HEAD_DIM).reshape(1, D_INNER)
    cfw = jnp.zeros((32, D_MODEL), F32).at[0:CF_KERNEL].set(cf_conv_w[0])

    hc = _normmod(ctx, norm_mix[0], cx(sc1), cx(sh1), tm=ctx_len)
    zero_state = jnp.zeros((bsz, D_STATE, D_INNER), F32)
    _, _, _, hf, hb = _front(hc, bsz, ctx_len, w, wdt4, cw, cb, dtb4, alog4, r3, r2,
                             zero_state, zero_state)

    x2d = x.reshape(t, D_MODEL)
    hx = _normmod(x, norm_mix[0], lat(sc1), lat(sh1), tm=1024)
    xbc, yf, yb, _, _ = _front(hx, bsz, seq, w, wdt4, cw, cb, dtb4, alog4, r3, r2, hf, hb)
    sz, sg = _mm_zg(hx, w, wgates)
    glu = _mm_glu(hx, wglu_u, wglu_v)

    ya = _gate_proja(yf.reshape(t, D_INNER), yb.reshape(t, D_INNER), xbc.reshape(t, CONV_DIM), sz,
                     dsk, ssm_norm[0].reshape(1, D_INNER), w_proj_a[0].astype(BF16))
    cf = _cfconv(glu.reshape(bsz, seq, D_MODEL), cfw, cf_conv_b[0].reshape(1, D_MODEL))
    x1, hx2 = _merge(cf.reshape(t, D_MODEL), ya, sg, x2d, cf_ln_g[0].reshape(1, D_MODEL),
                     cf_ln_b[0].reshape(1, D_MODEL), w_proj_b[0].astype(BF16), w_out[0].astype(BF16),
                     lat(g1), norm_ffn[0].reshape(1, D_MODEL), lat(sc2), lat(sh2), seq)
    h1 = _ffn_up(hx2, w_ffn_gate[0], w_ffn_up[0])
    out = _ffn_down(h1, w_ffn_down[0].astype(BF16), x1, lat(g2), norm_final.reshape(1, D_MODEL), seq)
    return out.reshape(bsz, seq, D_MODEL)
```

```python
import jax
import jax.numpy as jnp
import numpy as np
from jax import lax
from jax.experimental import pallas as pl
from jax.experimental.pallas import tpu as pltpu

F32 = jnp.float32
BF16 = jnp.bfloat16

D_MODEL = 2048
GRID_W = 64
D_INNER = 2 * D_MODEL
HEAD_DIM = 64
N_HEADS = D_INNER // HEAD_DIM
N_GROUPS = 8
GROUP_W = D_INNER // N_GROUPS
D_STATE = 128
SSM_CONV = 5
CHUNK = 128
BC_DIM = N_GROUPS * D_STATE
CONV_DIM = D_INNER + 2 * BC_DIM
CF_KERNEL = 31
D_FF = 5632
EPS = 1e-6
OFF_Z = D_INNER
OFF_XBC = OFF_Z + CONV_DIM
OFF_DT = OFF_XBC + 2 * N_HEADS
OFF_GLU = OFF_DT + 2 * D_MODEL

LANES = 128
HALF = LANES // 2
MXU_N = 256
VMEM_LIMIT = 56 * 1024 * 1024
NEG_BIG = -1e30
HALO = 16
LOG2E = 1.4426950408889634


def _cparams(*sem):
    return pltpu.CompilerParams(dimension_semantics=sem, vmem_limit_bytes=VMEM_LIMIT)


def _sigmoid(v):
    return jax.nn.sigmoid(v)


def _silu(v):
    return v * jax.nn.sigmoid(v)


def _rms(v, w):
    return v * lax.rsqrt(jnp.mean(v * v, axis=-1, keepdims=True) + EPS) * w


def _cast_weight_once(w_ref, wbf_ref):
    @pl.when(pl.program_id(1) == 0)
    def _():
        wbf_ref[...] = w_ref[...].astype(BF16)


def _ada_kernel(c_ref, w_ref, b_ref, o_ref):
    s = _silu(c_ref[...]).astype(BF16)
    o_ref[...] = jnp.dot(s, w_ref[...].astype(BF16), preferred_element_type=F32) + b_ref[...]


def _ada(cc, w_mod, b_mod):
    n = w_mod.shape[1]
    tn = 1024
    return pl.pallas_call(
        _ada_kernel,
        grid=(n // tn,),
        in_specs=[pl.BlockSpec((8, D_MODEL), lambda j: (0, 0)),
                  pl.BlockSpec((D_MODEL, tn), lambda j: (0, j)),
                  pl.BlockSpec((1, tn), lambda j: (0, j))],
        out_specs=pl.BlockSpec((8, tn), lambda j: (0, j)),
        out_shape=jax.ShapeDtypeStruct((8, n), F32),
        compiler_params=_cparams("arbitrary"),
        name="ada_params",
    )(cc, w_mod, b_mod.reshape(1, n))


def _normmod_kernel(xp_ref, x_ref, xn_ref, w_ref, sc_ref, sh_ref, o_ref):
    i = pl.program_id(1)
    n = pl.num_programs(1)
    tm = x_ref.shape[1]

    def nm(v):
        return _rms(v, w_ref[...]) * (1.0 + sc_ref[0]) + sh_ref[0]

    o_ref[0, 0:tm, :] = nm(x_ref[0]).astype(o_ref.dtype)
    nxt = jnp.where(i < n - 1, nm(xn_ref[0]), 0.0)
    prv = jnp.where(i > 0, nm(xp_ref[0]), 0.0)
    o_ref[0, tm:tm + HALO, :] = jnp.concatenate([nxt, prv], axis=0).astype(o_ref.dtype)


def _normmod(x3d, w, sc, sh, tm):
    bsz, seq, _ = x3d.shape
    nt = seq // tm
    r8 = tm // 8
    n8 = seq // 8
    mod = pl.BlockSpec((1, 1, D_MODEL), lambda b, i: (b, 0, 0))
    return pl.pallas_call(
        _normmod_kernel,
        grid=(bsz, nt),
        in_specs=[pl.BlockSpec((1, 8, D_MODEL), lambda b, i: (b, jnp.maximum(i * r8 - 1, 0), 0)),
                  pl.BlockSpec((1, tm, D_MODEL), lambda b, i: (b, i, 0)),
                  pl.BlockSpec((1, 8, D_MODEL), lambda b, i: (b, jnp.minimum((i + 1) * r8, n8 - 1), 0)),
                  pl.BlockSpec((1, D_MODEL), lambda b, i: (0, 0)), mod, mod],
        out_specs=pl.BlockSpec((1, tm + HALO, D_MODEL), lambda b, i: (b * nt + i, 0, 0)),
        out_shape=jax.ShapeDtypeStruct((bsz * nt, tm + HALO, D_MODEL), BF16),
        compiler_params=_cparams("arbitrary", "arbitrary"),
        name="norm_modulate",
    )(x3d, x3d, x3d, w.reshape(1, D_MODEL), sc, sh)


def _inproj_kernel(a_ref, wz_ref, wg_ref, wu_ref, wv_ref, oz_ref, og_ref, oglu_ref, wzbf_ref):
    _cast_weight_once(wz_ref, wzbf_ref)
    a = a_ref[0]
    oz_ref[...] = _silu(jnp.dot(a, wzbf_ref[...], preferred_element_type=F32)).astype(oz_ref.dtype)
    og_ref[...] = _sigmoid(jnp.dot(a, wg_ref[...], preferred_element_type=F32)).astype(og_ref.dtype)
    sv = _sigmoid(jnp.dot(a, wv_ref[...], preferred_element_type=F32))
    u = jnp.dot(a, wu_ref[...], preferred_element_type=F32)
    oglu_ref[...] = (u * sv).astype(oglu_ref.dtype)


def _mm_inproj(a3, w_in, wg, wu, wv, tn=512):
    ntile, tme, k = a3.shape
    tm = tme - HALO
    n = wg.shape[1]
    tg = tn * wu.shape[1] // n
    wspec = lambda w: pl.BlockSpec((k, w), lambda j, i: (0, j))
    ospec = lambda w: pl.BlockSpec((tm, w), lambda j, i: (i, j))
    return pl.pallas_call(
        _inproj_kernel,
        grid=(n // tn, ntile),
        in_specs=[pl.BlockSpec((1, tm, k), lambda j, i: (i, 0, 0)),
                  wspec(tn), wspec(tn), wspec(tg), wspec(tg)],
        out_specs=(ospec(tn), ospec(tn), ospec(tg)),
        out_shape=(jax.ShapeDtypeStruct((ntile * tm, n), BF16),
                   jax.ShapeDtypeStruct((ntile * tm, n), BF16),
                   jax.ShapeDtypeStruct((ntile * tm, wu.shape[1]), BF16)),
        scratch_shapes=[pltpu.VMEM((k, tn), BF16)],
        compiler_params=_cparams("arbitrary", "arbitrary"),
        name="proj_z_gates_glu",
    )(a3, w_in, wg, wu, wv)


def _dt_kernel(a_ref, w_ref, bias_ref, alog_ref, dt_ref, da_ref):
    raw = jnp.dot(a_ref[0], w_ref[...], preferred_element_type=F32) + bias_ref[...]
    dt = jnp.maximum(raw, 0.0) + jnp.log1p(jnp.exp(-jnp.abs(raw)))
    dt_ref[...] = dt
    da_ref[...] = dt * (-jnp.exp(alog_ref[...])) * LOG2E


def _mm_dt(a3, w, bias, alog):
    ntile, tme, k = a3.shape
    tm = tme - HALO
    n = w.shape[1]
    vec = pl.BlockSpec((1, n), lambda i: (0, 0))
    out = pl.BlockSpec((tm, n), lambda i: (i, 0))
    shape = jax.ShapeDtypeStruct((ntile * tm, n), F32)
    return pl.pallas_call(
        _dt_kernel,
        grid=(ntile,),
        in_specs=[pl.BlockSpec((1, tm, k), lambda i: (i, 0, 0)),
                  pl.BlockSpec((k, n), lambda i: (0, 0)), vec, vec],
        out_specs=(out, out),
        out_shape=(shape, shape),
        compiler_params=_cparams("arbitrary"),
        name="proj_dt",
    )(a3, w, bias, alog)


def _xbc_kernel(a_ref, w_ref, cw_ref, cb_ref, o_ref, wbf_ref):
    _cast_weight_once(w_ref, wbf_ref)
    a = a_ref[0]
    n = a.shape[0]
    tm = n - HALO
    for nb in range(o_ref.shape[1] // MXU_N):
        cs = slice(nb * MXU_N, (nb + 1) * MXU_N)
        p = jnp.dot(a, wbf_ref[:, cs], preferred_element_type=F32)
        acc = cb_ref[:, cs] + cw_ref[2:3, cs] * p
        for k in (0, 1, 3, 4):
            acc = acc + cw_ref[k:k + 1, cs] * pltpu.roll(p, (2 - k) % n, 0)
        o_ref[:, cs] = _silu(acc[0:tm]).astype(o_ref.dtype)


def _mm_xbc(a3, w_in, cw, cb, tn=1024):
    ntile, tme, k = a3.shape
    tm = tme - HALO
    n = CONV_DIM
    col0 = OFF_Z // tn
    return pl.pallas_call(
        _xbc_kernel,
        grid=(n // tn, ntile),
        in_specs=[pl.BlockSpec((1, tme, k), lambda j, i: (i, 0, 0)),
                  pl.BlockSpec((k, tn), lambda j, i: (0, col0 + j)),
                  pl.BlockSpec((8, tn), lambda j, i: (0, j)),
                  pl.BlockSpec((1, tn), lambda j, i: (0, j))],
        out_specs=pl.BlockSpec((tm, tn), lambda j, i: (i, j)),
        out_shape=jax.ShapeDtypeStruct((ntile * tm, n), BF16),
        scratch_shapes=[pltpu.VMEM((k, tn), BF16)],
        compiler_params=_cparams("arbitrary", "arbitrary"),
        name="proj_xbc_conv",
    )(a3, w_in, cw, cb)


def _ssd_prologue(xs_ref, b_ref, c_ref, dt_ref, da_ref, r3_ref, r2_ref,
                  ecs_ref, xdt_ref, xw_ref, row_ref, cb_ref, reverse):
    q = CHUNK
    da = da_ref[0]
    ri = lax.broadcasted_iota(jnp.int32, (q, q), 0)
    ci = lax.broadcasted_iota(jnp.int32, (q, q), 1)
    tri = (ri <= ci) if reverse else (ri >= ci)
    cs = jnp.dot(tri.astype(F32), da, precision=lax.Precision.HIGHEST,
                 preferred_element_type=F32)
    cst = cs.T
    lane_h = lax.broadcasted_iota(jnp.int32, (N_HEADS // 2, LANES), 1) < HALF
    ev = cst[0:N_HEADS // 2]
    od = cst[N_HEADS // 2:N_HEADS]
    row_ref[0] = jnp.where(lane_h, ev, pltpu.roll(od, HALF, 1))
    row_ref[1] = jnp.where(lane_h, pltpu.roll(ev, HALF, 1), od)

    lane_q = lax.broadcasted_iota(jnp.int32, (q, LANES), 1)
    lo_half = lane_q < HALF
    hi = cs.astype(BF16).astype(F32)
    r1 = cs - hi
    mid = r1.astype(BF16).astype(F32)
    lo = (r1 - mid).astype(BF16).astype(F32)
    x1 = jnp.where(lo_half, hi, pltpu.roll(mid, HALF, 1))
    x2 = jnp.where(lo_half, lo, 0.0)
    cparts = jnp.concatenate([x1, x2], axis=1).astype(BF16)
    dt = dt_ref[0]
    dhi = dt.astype(BF16)
    dmid = (dt - dhi.astype(F32)).astype(BF16)
    dparts = jnp.concatenate([dhi, dmid], axis=1)

    last = 0 if reverse else q - 1
    for g in range(N_GROUPS):
        gs = slice(g * GROUP_W, (g + 1) * GROUP_W)
        ecs = jnp.dot(cparts, r3_ref[:, gs], preferred_element_type=F32)
        ecs_ref[:, gs] = ecs
        edt = jnp.dot(dparts, r2_ref[:, gs], preferred_element_type=F32)
        xdt = xs_ref[0, :, gs].astype(F32) * edt
        xdt_ref[:, gs] = xdt.astype(BF16)
        xw_ref[:, gs] = (xdt * jnp.exp2(ecs[last:last + 1, :] - ecs)).astype(BF16)

    for g in range(N_GROUPS):
        cg = c_ref[0, :, g * D_STATE:(g + 1) * D_STATE]
        bg = b_ref[0, :, g * D_STATE:(g + 1) * D_STATE]
        cb = lax.dot_general(cg, bg, (((1,), (1,)), ((), ())), preferred_element_type=F32)
        cbr = pltpu.roll(cb, HALF, 1)
        cb_ref[g, 0] = jnp.where(lo_half, cb, cbr)
        cb_ref[g, 1] = jnp.where(lo_half, cbr, cb)


def _ssd_main(b_ref, c_ref, y_ref, h_ref, ecs_ref, xdt_ref, xw_ref, row_ref, cb_ref, reverse):
    q = CHUNK
    last = 0 if reverse else q - 1
    lane_q = lax.broadcasted_iota(jnp.int32, (q, LANES), 1)
    lo_half = lane_q < HALF
    pos = lax.broadcasted_iota(jnp.int32, (q, LANES), 0)
    s0 = jnp.where(lo_half, lane_q, lane_q - HALF)
    s1 = s0 + HALF
    m0 = (pos <= s0) if reverse else (pos >= s0)
    m1 = (pos <= s1) if reverse else (pos >= s1)
    lo64 = lax.broadcasted_iota(jnp.int32, (HALF, LANES), 1) < HALF
    zero = jnp.zeros((HALF, LANES), BF16)

    for g in range(N_GROUPS):
        gs = slice(g * GROUP_W, (g + 1) * GROUP_W)
        cg = c_ref[0, :, g * D_STATE:(g + 1) * D_STATE]
        bg = b_ref[0, :, g * D_STATE:(g + 1) * D_STATE]
        cb0 = cb_ref[g, 0]
        cb1 = cb_ref[g, 1]
        hg = h_ref[0, :, gs]
        yoff = jnp.dot(cg, hg.astype(BF16), preferred_element_type=F32)
        for p in range(GROUP_W // LANES):
            k = g * (GROUP_W // LANES) + p
            sl = slice(k * LANES, (k + 1) * LANES)
            a = ecs_ref[:, sl]
            t0 = jnp.exp2(jnp.where(m0, a - row_ref[0, k:k + 1, :], NEG_BIG)) * cb0
            t1 = jnp.exp2(jnp.where(m1, a - row_ref[1, k:k + 1, :], NEG_BIG)) * cb1
            lhs = jnp.concatenate([t0, t1], axis=1).astype(BF16)
            xa = xdt_ref[0:HALF, sl]
            xb = xdt_ref[HALF:q, sl]
            rhs = jnp.concatenate([jnp.where(lo64, xa, zero), jnp.where(lo64, zero, xa),
                                   jnp.where(lo64, xb, zero), jnp.where(lo64, zero, xb)], axis=0)
            yd = jnp.dot(lhs, rhs, preferred_element_type=F32)
            y = yd + yoff[:, p * LANES:(p + 1) * LANES] * jnp.exp2(a)
            y_ref[0, :, sl] = y.astype(y_ref.dtype)
        st = lax.dot_general(bg, xw_ref[:, gs], (((0,), (0,)), ((), ())), preferred_element_type=F32)
        h_ref[0, :, gs] = hg * jnp.exp2(ecs_ref[last:last + 1, gs]) + st


def _ssd_kernel(xsf_ref, bf_ref, cf_ref, dtf_ref, daf_ref, xsb_ref, bb_ref, cbk_ref, dtb_ref, dab_ref,
                r3_ref, r2_ref, h0f_ref, h0b_ref, yf_ref, yb_ref, hf_ref, hb_ref,
                ecs_ref, xdt_ref, xw_ref, row_ref, cb_ref):
    @pl.when(pl.program_id(1) == 0)
    def _():
        hf_ref[...] = h0f_ref[...]
        hb_ref[...] = h0b_ref[...]

    sc = lambda d: (ecs_ref.at[d], xdt_ref.at[d], xw_ref.at[d], row_ref.at[d], cb_ref.at[d])
    _ssd_prologue(xsf_ref, bf_ref, cf_ref, dtf_ref, daf_ref, r3_ref, r2_ref, *sc(0), reverse=False)
    _ssd_prologue(xsb_ref, bb_ref, cbk_ref, dtb_ref, dab_ref, r3_ref, r2_ref, *sc(1), reverse=True)
    _ssd_main(bf_ref, cf_ref, yf_ref, hf_ref, *sc(0), reverse=False)
    _ssd_main(bb_ref, cbk_ref, yb_ref, hb_ref, *sc(1), reverse=True)


def _ssd(xbc, dt, da, r3, r2, h0f, h0b):
    bsz, seq, _ = xbc.shape
    nc = seq // CHUNK
    q = CHUNK
    bcol = D_INNER // BC_DIM
    fwd = lambda c: c
    bwd = lambda c: nc - 1 - c
    const = lambda r, c: pl.BlockSpec((r, c), lambda b, cc: (0, 0), pipeline_mode=pl.Buffered(1))
    state = pl.BlockSpec((1, D_STATE, D_INNER), lambda b, c: (b, 0, 0))

    def chunk_specs(ci, dcol):
        return [pl.BlockSpec((1, q, D_INNER), lambda b, c: (b, ci(c), 0)),
                pl.BlockSpec((1, q, BC_DIM), lambda b, c: (b, ci(c), bcol)),
                pl.BlockSpec((1, q, BC_DIM), lambda b, c: (b, ci(c), bcol + 1)),
                pl.BlockSpec((1, q, LANES), lambda b, c: (b, ci(c), dcol)),
                pl.BlockSpec((1, q, LANES), lambda b, c: (b, ci(c), dcol))]

    return pl.pallas_call(
        _ssd_kernel,
        grid=(bsz, nc),
        in_specs=chunk_specs(fwd, 0) + chunk_specs(bwd, 1)
        + [const(2 * LANES, D_INNER), const(2 * LANES, D_INNER), state, state],
        out_specs=(pl.BlockSpec((1, q, D_INNER), lambda b, c: (b, fwd(c), 0)),
                   pl.BlockSpec((1, q, D_INNER), lambda b, c: (b, bwd(c), 0)),
                   state, state),
        out_shape=(jax.ShapeDtypeStruct((bsz, seq, D_INNER), BF16),
                   jax.ShapeDtypeStruct((bsz, seq, D_INNER), BF16),
                   jax.ShapeDtypeStruct((bsz, D_STATE, D_INNER), F32),
                   jax.ShapeDtypeStruct((bsz, D_STATE, D_INNER), F32)),
        scratch_shapes=[pltpu.VMEM((2, q, D_INNER), F32),
                        pltpu.VMEM((2, q, D_INNER), BF16),
                        pltpu.VMEM((2, q, D_INNER), BF16),
                        pltpu.VMEM((2, 2, N_HEADS // 2, LANES), F32),
                        pltpu.VMEM((2, N_GROUPS, 2, q, LANES), F32)],
        compiler_params=_cparams("arbitrary", "arbitrary"),
        name="ssd_scan",
    )(xbc, xbc, xbc, dt, da, xbc, xbc, xbc, dt, da, r3, r2, h0f, h0b)


def _gate_proja_kernel(yf_ref, yb_ref, xs_ref, sz_ref, dsk_ref, nw_ref, w_ref, o_ref):
    y = (yf_ref[...].astype(F32) + yb_ref[...].astype(F32)
         + dsk_ref[...] * xs_ref[...].astype(F32))
    u = y * sz_ref[...].astype(F32)
    un = _rms(u, nw_ref[...]).astype(BF16)
    o_ref[...] = jnp.dot(un, w_ref[...], preferred_element_type=F32).astype(o_ref.dtype)


def _gate_proja(yf, yb, xbc, sz, dsk, nw, w, tm=256):
    t = yf.shape[0]
    tok = pl.BlockSpec((tm, D_INNER), lambda i: (i, 0))
    vec = pl.BlockSpec((1, D_INNER), lambda i: (0, 0))
    return pl.pallas_call(
        _gate_proja_kernel,
        grid=(t // tm,),
        in_specs=[tok, tok, tok, tok, vec, vec,
                  pl.BlockSpec((D_INNER, D_MODEL), lambda i: (0, 0), pipeline_mode=pl.Buffered(1))],
        out_specs=pl.BlockSpec((tm, D_MODEL), lambda i: (i, 0)),
        out_shape=jax.ShapeDtypeStruct((t, D_MODEL), BF16),
        compiler_params=_cparams("arbitrary"),
        name="gate_proj_a",
    )(yf, yb, xbc, sz, dsk, nw, w)


CF_TT = 2048
CF_HALO = 1024
CF_CB = 256


def _cfconv_kernel(gp_ref, gc_ref, gn_ref, w_ref, b_ref, o_ref, win_ref):
    r = pl.program_id(2)
    n = pl.num_programs(2)
    win_ref[0:CF_HALO, :] = jnp.where(r > 0, gp_ref[0].astype(F32), 0.0)
    win_ref[CF_HALO:CF_HALO + CF_TT, :] = gc_ref[0].astype(F32)
    win_ref[CF_HALO + CF_TT:2 * CF_HALO + CF_TT, :] = jnp.where(r < n - 1, gn_ref[0].astype(F32), 0.0)
    half = CF_KERNEL // 2

    def body(rr, carry):
        base = pl.multiple_of(CF_HALO + 2 * rr * GRID_W - half * GRID_W, GRID_W)
        acc0 = jnp.zeros((GRID_W, CF_CB), F32) + b_ref[...]
        acc1 = acc0
        for j in range(CF_KERNEL + 1):
            xrow = win_ref[pl.ds(base + j * GRID_W, GRID_W), :]
            if j < CF_KERNEL:
                acc0 = acc0 + w_ref[j:j + 1, :] * xrow
            if j > 0:
                acc1 = acc1 + w_ref[j - 1:j, :] * xrow
        out0 = pl.multiple_of(2 * rr * GRID_W, GRID_W)
        o_ref[0, pl.ds(out0, GRID_W), :] = acc0.astype(o_ref.dtype)
        o_ref[0, pl.ds(pl.multiple_of(out0 + GRID_W, GRID_W), GRID_W), :] = acc1.astype(o_ref.dtype)
        return carry

    lax.fori_loop(0, CF_TT // (2 * GRID_W), body, 0)


def _cfconv(g, w, b):
    bsz, seq, ch = g.shape
    nt = seq // CF_TT
    hpt = CF_TT // CF_HALO
    nh = seq // CF_HALO
    blk = lambda f: pl.BlockSpec((1, CF_TT, CF_CB), f)
    hblk = lambda f: pl.BlockSpec((1, CF_HALO, CF_CB), f)
    return pl.pallas_call(
        _cfconv_kernel,
        grid=(bsz, ch // CF_CB, nt),
        in_specs=[hblk(lambda bb, c, r: (bb, jnp.maximum(r * hpt - 1, 0), c)),
                  blk(lambda bb, c, r: (bb, r, c)),
                  hblk(lambda bb, c, r: (bb, jnp.minimum((r + 1) * hpt, nh - 1), c)),
                  pl.BlockSpec((32, CF_CB), lambda bb, c, r: (0, c)),
                  pl.BlockSpec((1, CF_CB), lambda bb, c, r: (0, c))],
        out_specs=blk(lambda bb, c, r: (bb, r, c)),
        out_shape=jax.ShapeDtypeStruct((bsz, seq, ch), BF16),
        scratch_shapes=[pltpu.VMEM((CF_TT + 2 * CF_HALO, CF_CB), F32)],
        compiler_params=_cparams("arbitrary", "arbitrary", "arbitrary"),
        name="conformer_conv",
    )(g, g, g, w, b)


def _merge_kernel(cf_ref, ya_ref, sg_ref, x_ref, lng_ref, lnb_ref, wpb_ref, wout_ref, g1_ref,
                  nw_ref, sc_ref, sh_ref, o_ref, hx_ref):
    cf = cf_ref[...].astype(F32)
    mu = jnp.mean(cf, axis=-1, keepdims=True)
    xc = cf - mu
    var = jnp.mean(xc * xc, axis=-1, keepdims=True)
    ln = xc * lax.rsqrt(var + EPS) * lng_ref[...] + lnb_ref[...]
    yb = jnp.dot(_silu(ln).astype(BF16), wpb_ref[...], preferred_element_type=F32)
    m = (sg_ref[:, 0:D_MODEL].astype(F32) * ya_ref[...].astype(F32)
         + sg_ref[:, D_MODEL:2 * D_MODEL].astype(F32) * yb)
    mix = jnp.dot(m.astype(BF16), wout_ref[...], preferred_element_type=F32)
    x1 = x_ref[...] + g1_ref[0] * mix
    o_ref[...] = x1
    hx_ref[...] = (_rms(x1, nw_ref[...]) * (1.0 + sc_ref[0]) + sh_ref[0]).astype(hx_ref.dtype)


def _merge(cf, ya, sg, x2d, lng, lnb, wpb, wout, g1, nw, sc2, sh2, tokens_per_batch, tm=256):
    t = x2d.shape[0]
    tpb = tokens_per_batch // tm
    tok = lambda w: pl.BlockSpec((tm, w), lambda i: (i, 0))
    vec = pl.BlockSpec((1, D_MODEL), lambda i: (0, 0))
    mod = pl.BlockSpec((1, 1, D_MODEL), lambda i: (i // tpb, 0, 0))
    wspec = pl.BlockSpec((D_MODEL, D_MODEL), lambda i: (0, 0), pipeline_mode=pl.Buffered(1))
    return pl.pallas_call(
        _merge_kernel,
        grid=(t // tm,),
        in_specs=[tok(D_MODEL), tok(D_MODEL), tok(2 * D_MODEL), tok(D_MODEL), vec, vec, wspec, wspec,
                  mod, vec, mod, mod],
        out_specs=(tok(D_MODEL), tok(D_MODEL)),
        out_shape=(jax.ShapeDtypeStruct((t, D_MODEL), F32), jax.ShapeDtypeStruct((t, D_MODEL), BF16)),
        compiler_params=_cparams("arbitrary"),
        name="merge_out_proj",
    )(cf, ya, sg, x2d, lng, lnb, wpb, wout, g1, nw, sc2, sh2)


def _ffn_up_kernel(a_ref, wg_ref, wu_ref, o_ref, wgbf_ref, wubf_ref):
    _cast_weight_once(wg_ref, wgbf_ref)
    _cast_weight_once(wu_ref, wubf_ref)
    a = a_ref[...]
    gate = jnp.dot(a, wgbf_ref[...], preferred_element_type=F32)
    up = jnp.dot(a, wubf_ref[...], preferred_element_type=F32)
    o_ref[...] = (_silu(gate) * up).astype(o_ref.dtype)


def _ffn_up(hx, wg, wu, tm=1024, tn=512):
    t, k = hx.shape
    n = wg.shape[1]
    return pl.pallas_call(
        _ffn_up_kernel,
        grid=(n // tn, t // tm),
        in_specs=[pl.BlockSpec((tm, k), lambda j, i: (i, 0)),
                  pl.BlockSpec((k, tn), lambda j, i: (0, j)),
                  pl.BlockSpec((k, tn), lambda j, i: (0, j))],
        out_specs=pl.BlockSpec((tm, tn), lambda j, i: (i, j)),
        out_shape=jax.ShapeDtypeStruct((t, n), BF16),
        scratch_shapes=[pltpu.VMEM((k, tn), BF16), pltpu.VMEM((k, tn), BF16)],
        compiler_params=_cparams("arbitrary", "arbitrary"),
        name="ffn_up",
    )(hx, wg, wu)


def _ffn_down_kernel(h_ref, w_ref, x_ref, g2_ref, nf_ref, o_ref):
    acc = jnp.dot(h_ref[...], w_ref[...], preferred_element_type=F32)
    o_ref[...] = _rms(x_ref[...] + g2_ref[0] * acc, nf_ref[...])


def _ffn_down(h1, wd, x1, g2, nf, tokens_per_batch, tm=512):
    t = x1.shape[0]
    tpb = tokens_per_batch // tm
    return pl.pallas_call(
        _ffn_down_kernel,
        grid=(t // tm,),
        in_specs=[pl.BlockSpec((tm, D_FF), lambda i: (i, 0)),
                  pl.BlockSpec((D_FF, D_MODEL), lambda i: (0, 0), pipeline_mode=pl.Buffered(1)),
                  pl.BlockSpec((tm, D_MODEL), lambda i: (i, 0)),
                  pl.BlockSpec((1, 1, D_MODEL), lambda i: (i // tpb, 0, 0)),
                  pl.BlockSpec((1, D_MODEL), lambda i: (0, 0))],
        out_specs=pl.BlockSpec((tm, D_MODEL), lambda i: (i, 0)),
        out_shape=jax.ShapeDtypeStruct((t, D_MODEL), F32),
        compiler_params=_cparams("arbitrary"),
        name="ffn_down",
    )(h1, wd, x1, g2, nf)


_HEAD_PERM = np.concatenate([np.arange(0, N_HEADS, 2), np.arange(1, N_HEADS, 2)])


def _expand_matrix(row_blocks):
    col_head = np.arange(D_INNER) // HEAD_DIM
    pat = (_HEAD_PERM[:, None] == col_head[None, :]).astype(np.float32)
    zero = np.zeros_like(pat)
    return jnp.asarray(np.concatenate([pat if on else zero for on in row_blocks], axis=0), BF16)


def _front(h3, bsz, seq, w_in, wdt4, cw, cb, dtb4, alog4, r3, r2, h0f, h0b):
    xbc = _mm_xbc(h3, w_in, cw, cb).reshape(bsz, seq, CONV_DIM)
    dt, da = _mm_dt(h3, wdt4, dtb4, alog4)
    dt = dt.reshape(bsz, seq, 2 * LANES)
    da = da.reshape(bsz, seq, 2 * LANES)
    yf, yb, hf, hb = _ssd(xbc, dt, da, r3, r2, h0f, h0b)
    return xbc, yf, yb, hf, hb


@jax.jit
def kernel(x, c, ctx, c_ctx, w_mod, b_mod, norm_mix, w_in, ssm_conv_w, ssm_conv_b, dt_bias, a_log,
           d_skip, ssm_norm, cf_conv_w, cf_conv_b, cf_ln_g, cf_ln_b, w_proj_a, w_proj_b, w_out,
           norm_ffn, w_ffn_gate, w_ffn_up, w_ffn_down, norm_final):
    bsz, seq, _ = x.shape
    ctx_len = ctx.shape[1]
    t = bsz * seq

    cc = jnp.zeros((8, D_MODEL), F32).at[0:bsz].set(c).at[bsz].set(c_ctx)
    mod = _ada(cc, w_mod[0], b_mod[0])
    sh1, sc1, g1, sh2, sc2, g2 = [m.reshape(8, 1, D_MODEL) for m in jnp.split(mod, 6, axis=-1)]
    lat = lambda m: m[0:bsz]
    cx = lambda m: jnp.broadcast_to(m[bsz:bsz + 1], (bsz, 1, D_MODEL))

    w = w_in[0]
    wdt = w[:, OFF_XBC:OFF_DT]
    wdt_f = wdt[:, 0:N_HEADS][:, _HEAD_PERM]
    wdt_b = wdt[:, N_HEADS:2 * N_HEADS][:, _HEAD_PERM]
    wdt4 = jnp.concatenate([wdt_f, wdt_b, wdt_b, wdt_f], axis=1).astype(BF16)
    wglu_u = w[:, OFF_DT:OFF_DT + D_MODEL].astype(BF16)
    wglu_v = w[:, OFF_DT + D_MODEL:OFF_GLU].astype(BF16)
    wgates = w[:, OFF_GLU:].astype(BF16)
    perm2 = lambda p: jnp.concatenate([p[0][_HEAD_PERM], p[1][_HEAD_PERM],
                                       p[1][_HEAD_PERM], p[0][_HEAD_PERM]]).reshape(1, 2 * LANES)
    dtb4 = perm2(dt_bias[0])
    alog4 = perm2(a_log[0])
    cw = jnp.zeros((8, CONV_DIM), F32).at[0:SSM_CONV].set(ssm_conv_w[0])
    cb = ssm_conv_b[0].reshape(1, CONV_DIM)
    r3 = _expand_matrix([True, True, True, False])
    r2 = _expand_matrix([True, False, True, False])
    dsk = jnp.repeat(d_skip[0], HEAD_DIM).reshape(1, D_INNER)
    cfw = jnp.zeros((32, D_MODEL), F32).at[0:CF_KERNEL].set(cf_conv_w[0])

    hc = _normmod(ctx, norm_mix[0], cx(sc1), cx(sh1), tm=ctx_len)
    zero_state = jnp.zeros((bsz, D_STATE, D_INNER), F32)
    _, _, _, hf, hb = _front(hc, bsz, ctx_len, w, wdt4, cw, cb, dtb4, alog4, r3, r2,
                             zero_state, zero_state)

    x2d = x.reshape(t, D_MODEL)
    hx = _normmod(x, norm_mix[0], lat(sc1), lat(sh1), tm=1024)
    xbc, yf, yb, _, _ = _front(hx, bsz, seq, w, wdt4, cw, cb, dtb4, alog4, r3, r2, hf, hb)
    sz, sg, glu = _mm_inproj(hx, w, wgates, wglu_u, wglu_v)

    ya = _gate_proja(yf.reshape(t, D_INNER), yb.reshape(t, D_INNER), xbc.reshape(t, CONV_DIM), sz,
                     dsk, ssm_norm[0].reshape(1, D_INNER), w_proj_a[0].astype(BF16))
    cf = _cfconv(glu.reshape(bsz, seq, D_MODEL), cfw, cf_conv_b[0].reshape(1, D_MODEL))
    x1, hx2 = _merge(cf.reshape(t, D_MODEL), ya, sg, x2d, cf_ln_g[0].reshape(1, D_MODEL),
                     cf_ln_b[0].reshape(1, D_MODEL), w_proj_b[0].astype(BF16), w_out[0].astype(BF16),
                     lat(g1), norm_ffn[0].reshape(1, D_MODEL), lat(sc2), lat(sh2), seq)
    h1 = _ffn_up(hx2, w_ffn_gate[0], w_ffn_up[0])
    out = _ffn_down(h1, w_ffn_down[0].astype(BF16), x1, lat(g2), norm_final.reshape(1, D_MODEL), seq)
    return out.reshape(bsz, seq, D_MODEL)
```

```python
import jax
import jax.numpy as jnp
import numpy as np
from jax import lax
from jax.experimental import pallas as pl
from jax.experimental.pallas import tpu as pltpu

F32 = jnp.float32
BF16 = jnp.bfloat16

D_MODEL = 2048
GRID_W = 64
D_INNER = 2 * D_MODEL
HEAD_DIM = 64
N_HEADS = D_INNER // HEAD_DIM
N_GROUPS = 8
GROUP_W = D_INNER // N_GROUPS
D_STATE = 128
SSM_CONV = 5
CHUNK = 128
BC_DIM = N_GROUPS * D_STATE
CONV_DIM = D_INNER + 2 * BC_DIM
CF_KERNEL = 31
D_FF = 5632
EPS = 1e-6
OFF_Z = D_INNER
OFF_XBC = OFF_Z + CONV_DIM
OFF_DT = OFF_XBC + 2 * N_HEADS
OFF_GLU = OFF_DT + 2 * D_MODEL

LANES = 128
HALF = LANES // 2
MXU_N = 256
SUB_N = 2 * MXU_N
VMEM_LIMIT = 56 * 1024 * 1024
NEG_BIG = -1e30
HALO = 16
LOG2E = 1.4426950408889634


def _cparams(*sem):
    return pltpu.CompilerParams(dimension_semantics=sem, vmem_limit_bytes=VMEM_LIMIT)


def _sigmoid(v):
    return jax.nn.sigmoid(v)


def _silu(v):
    return v * jax.nn.sigmoid(v)


def _rms(v, w):
    return v * lax.rsqrt(jnp.mean(v * v, axis=-1, keepdims=True) + EPS) * w


def _cast_weight_once(w_ref, wbf_ref):
    @pl.when(pl.program_id(1) == 0)
    def _():
        wbf_ref[...] = w_ref[...].astype(BF16)


def _ada_kernel(c_ref, w_ref, b_ref, o_ref):
    s = _silu(c_ref[...]).astype(BF16)
    o_ref[...] = jnp.dot(s, w_ref[...].astype(BF16), preferred_element_type=F32) + b_ref[...]


def _ada(cc, w_mod, b_mod):
    n = w_mod.shape[1]
    tn = 1024
    return pl.pallas_call(
        _ada_kernel,
        grid=(n // tn,),
        in_specs=[pl.BlockSpec((8, D_MODEL), lambda j: (0, 0)),
                  pl.BlockSpec((D_MODEL, tn), lambda j: (0, j)),
                  pl.BlockSpec((1, tn), lambda j: (0, j))],
        out_specs=pl.BlockSpec((8, tn), lambda j: (0, j)),
        out_shape=jax.ShapeDtypeStruct((8, n), F32),
        compiler_params=_cparams("arbitrary"),
        name="ada_params",
    )(cc, w_mod, b_mod.reshape(1, n))


def _normmod_kernel(xp_ref, x_ref, xn_ref, w_ref, sc_ref, sh_ref, wdt_ref, bias_ref, alog_ref,
                    o_ref, dt_ref, da_ref):
    i = pl.program_id(1)
    n = pl.num_programs(1)
    tm = x_ref.shape[1]

    def nm(v):
        return _rms(v, w_ref[...]) * (1.0 + sc_ref[0]) + sh_ref[0]

    h = nm(x_ref[0]).astype(o_ref.dtype)
    o_ref[0, 0:tm, :] = h
    nxt = jnp.where(i < n - 1, nm(xn_ref[0]), 0.0)
    prv = jnp.where(i > 0, nm(xp_ref[0]), 0.0)
    o_ref[0, tm:tm + HALO, :] = jnp.concatenate([nxt, prv], axis=0).astype(o_ref.dtype)

    raw = jnp.dot(h, wdt_ref[...], preferred_element_type=F32) + bias_ref[...]
    dt = jnp.maximum(raw, 0.0) + jnp.log1p(jnp.exp(-jnp.abs(raw)))
    dt_ref[...] = dt
    da_ref[...] = dt * (-jnp.exp(alog_ref[...])) * LOG2E


def _normmod(x3d, w, sc, sh, wdt, bias, alog, tm):
    bsz, seq, _ = x3d.shape
    nt = seq // tm
    r8 = tm // 8
    n8 = seq // 8
    nd = wdt.shape[1]
    mod = pl.BlockSpec((1, 1, D_MODEL), lambda b, i: (b, 0, 0))
    vec = pl.BlockSpec((1, nd), lambda b, i: (0, 0))
    dspec = pl.BlockSpec((tm, nd), lambda b, i: (b * nt + i, 0))
    dshape = jax.ShapeDtypeStruct((bsz * seq, nd), F32)
    return pl.pallas_call(
        _normmod_kernel,
        grid=(bsz, nt),
        in_specs=[pl.BlockSpec((1, 8, D_MODEL), lambda b, i: (b, jnp.maximum(i * r8 - 1, 0), 0)),
                  pl.BlockSpec((1, tm, D_MODEL), lambda b, i: (b, i, 0)),
                  pl.BlockSpec((1, 8, D_MODEL), lambda b, i: (b, jnp.minimum((i + 1) * r8, n8 - 1), 0)),
                  pl.BlockSpec((1, D_MODEL), lambda b, i: (0, 0)), mod, mod,
                  pl.BlockSpec((D_MODEL, nd), lambda b, i: (0, 0)), vec, vec],
        out_specs=(pl.BlockSpec((1, tm + HALO, D_MODEL), lambda b, i: (b * nt + i, 0, 0)), dspec, dspec),
        out_shape=(jax.ShapeDtypeStruct((bsz * nt, tm + HALO, D_MODEL), BF16), dshape, dshape),
        compiler_params=_cparams("arbitrary", "arbitrary"),
        name="norm_modulate_dt",
    )(x3d, x3d, x3d, w.reshape(1, D_MODEL), sc, sh, wdt, bias, alog)


def _glu_kernel(a_ref, wu_ref, wv_ref, o_ref):
    a = a_ref[0]
    for nb in range(o_ref.shape[1] // SUB_N):
        cs = slice(nb * SUB_N, (nb + 1) * SUB_N)
        sv = _sigmoid(jnp.dot(a, wv_ref[:, cs], preferred_element_type=F32))
        u = jnp.dot(a, wu_ref[:, cs], preferred_element_type=F32)
        o_ref[:, cs] = (u * sv).astype(o_ref.dtype)


def _mm_glu(a3, wu, wv, tn=1024):
    ntile, tme, k = a3.shape
    tm = tme - HALO
    n = wu.shape[1]
    return pl.pallas_call(
        _glu_kernel,
        grid=(n // tn, ntile),
        in_specs=[pl.BlockSpec((1, tm, k), lambda j, i: (i, 0, 0)),
                  pl.BlockSpec((k, tn), lambda j, i: (0, j)),
                  pl.BlockSpec((k, tn), lambda j, i: (0, j))],
        out_specs=pl.BlockSpec((tm, tn), lambda j, i: (i, j)),
        out_shape=jax.ShapeDtypeStruct((ntile * tm, n), BF16),
        compiler_params=_cparams("arbitrary", "arbitrary"),
        name="proj_glu",
    )(a3, wu, wv)


def _zg_kernel(a_ref, wz_ref, wg_ref, oz_ref, og_ref, wzbf_ref):
    _cast_weight_once(wz_ref, wzbf_ref)
    a = a_ref[0]
    for nb in range(oz_ref.shape[1] // SUB_N):
        cs = slice(nb * SUB_N, (nb + 1) * SUB_N)
        oz_ref[:, cs] = _silu(jnp.dot(a, wzbf_ref[:, cs], preferred_element_type=F32)).astype(oz_ref.dtype)
        og_ref[:, cs] = _sigmoid(jnp.dot(a, wg_ref[:, cs], preferred_element_type=F32)).astype(og_ref.dtype)


def _mm_zg(a3, w_in, wg, tn=1024):
    ntile, tme, k = a3.shape
    tm = tme - HALO
    n = wg.shape[1]
    wspec = pl.BlockSpec((k, tn), lambda j, i: (0, j))
    ospec = pl.BlockSpec((tm, tn), lambda j, i: (i, j))
    shape = jax.ShapeDtypeStruct((ntile * tm, n), BF16)
    return pl.pallas_call(
        _zg_kernel,
        grid=(n // tn, ntile),
        in_specs=[pl.BlockSpec((1, tm, k), lambda j, i: (i, 0, 0)), wspec, wspec],
        out_specs=(ospec, ospec),
        out_shape=(shape, shape),
        scratch_shapes=[pltpu.VMEM((k, tn), BF16)],
        compiler_params=_cparams("arbitrary", "arbitrary"),
        name="proj_z_gates",
    )(a3, w_in, wg)


def _xbc_kernel(a_ref, w_ref, cw_ref, cb_ref, o_ref, wbf_ref):
    _cast_weight_once(w_ref, wbf_ref)
    a = a_ref[0]
    n = a.shape[0]
    tm = n - HALO
    for nb in range(o_ref.shape[1] // MXU_N):
        cs = slice(nb * MXU_N, (nb + 1) * MXU_N)
        p = jnp.dot(a, wbf_ref[:, cs], preferred_element_type=F32)
        acc = cb_ref[:, cs] + cw_ref[2:3, cs] * p
        for k in (0, 1, 3, 4):
            acc = acc + cw_ref[k:k + 1, cs] * pltpu.roll(p, (2 - k) % n, 0)
        o_ref[:, cs] = _silu(acc[0:tm]).astype(o_ref.dtype)


def _mm_xbc(a3, w_in, cw, cb, tn=1024):
    ntile, tme, k = a3.shape
    tm = tme - HALO
    n = CONV_DIM
    col0 = OFF_Z // tn
    return pl.pallas_call(
        _xbc_kernel,
        grid=(n // tn, ntile),
        in_specs=[pl.BlockSpec((1, tme, k), lambda j, i: (i, 0, 0)),
                  pl.BlockSpec((k, tn), lambda j, i: (0, col0 + j)),
                  pl.BlockSpec((8, tn), lambda j, i: (0, j)),
                  pl.BlockSpec((1, tn), lambda j, i: (0, j))],
        out_specs=pl.BlockSpec((tm, tn), lambda j, i: (i, j)),
        out_shape=jax.ShapeDtypeStruct((ntile * tm, n), BF16),
        scratch_shapes=[pltpu.VMEM((k, tn), BF16)],
        compiler_params=_cparams("arbitrary", "arbitrary"),
        name="proj_xbc_conv",
    )(a3, w_in, cw, cb)


def _ssd_prologue(xs_ref, b_ref, c_ref, dt_ref, da_ref, r3_ref, r2_ref,
                  ecs_ref, xdt_ref, xw_ref, row_ref, cb_ref, reverse):
    q = CHUNK
    da = da_ref[0]
    ri = lax.broadcasted_iota(jnp.int32, (q, q), 0)
    ci = lax.broadcasted_iota(jnp.int32, (q, q), 1)
    tri = (ri <= ci) if reverse else (ri >= ci)
    cs = jnp.dot(tri.astype(F32), da, precision=lax.Precision.HIGHEST,
                 preferred_element_type=F32)
    cst = cs.T
    lane_h = lax.broadcasted_iota(jnp.int32, (N_HEADS // 2, LANES), 1) < HALF
    ev = cst[0:N_HEADS // 2]
    od = cst[N_HEADS // 2:N_HEADS]
    row_ref[0] = jnp.where(lane_h, ev, pltpu.roll(od, HALF, 1))
    row_ref[1] = jnp.where(lane_h, pltpu.roll(ev, HALF, 1), od)

    lane_q = lax.broadcasted_iota(jnp.int32, (q, LANES), 1)
    lo_half = lane_q < HALF
    hi = cs.astype(BF16).astype(F32)
    r1 = cs - hi
    mid = r1.astype(BF16).astype(F32)
    lo = (r1 - mid).astype(BF16).astype(F32)
    x1 = jnp.where(lo_half, hi, pltpu.roll(mid, HALF, 1))
    x2 = jnp.where(lo_half, lo, 0.0)
    cparts = jnp.concatenate([x1, x2], axis=1).astype(BF16)
    dt = dt_ref[0]
    dhi = dt.astype(BF16)
    dmid = (dt - dhi.astype(F32)).astype(BF16)
    dparts = jnp.concatenate([dhi, dmid], axis=1)

    last = 0 if reverse else q - 1
    for g in range(N_GROUPS):
        gs = slice(g * GROUP_W, (g + 1) * GROUP_W)
        ecs = jnp.dot(cparts, r3_ref[:, gs], preferred_element_type=F32)
        ecs_ref[:, gs] = ecs
        edt = jnp.dot(dparts, r2_ref[:, gs], preferred_element_type=F32)
        xdt = xs_ref[0, :, gs].astype(F32) * edt
        xdt_ref[:, gs] = xdt.astype(BF16)
        xw_ref[:, gs] = (xdt * jnp.exp2(ecs[last:last + 1, :] - ecs)).astype(BF16)

    for g in range(N_GROUPS):
        cg = c_ref[0, :, g * D_STATE:(g + 1) * D_STATE]
        bg = b_ref[0, :, g * D_STATE:(g + 1) * D_STATE]
        cb = lax.dot_general(cg, bg, (((1,), (1,)), ((), ())), preferred_element_type=F32)
        cbr = pltpu.roll(cb, HALF, 1)
        cb_ref[g, 0] = jnp.where(lo_half, cb, cbr)
        cb_ref[g, 1] = jnp.where(lo_half, cbr, cb)


def _ssd_main(b_ref, c_ref, y_ref, h_ref, ecs_ref, xdt_ref, xw_ref, row_ref, cb_ref, reverse):
    q = CHUNK
    last = 0 if reverse else q - 1
    lane_q = lax.broadcasted_iota(jnp.int32, (q, LANES), 1)
    lo_half = lane_q < HALF
    pos = lax.broadcasted_iota(jnp.int32, (q, LANES), 0)
    s0 = jnp.where(lo_half, lane_q, lane_q - HALF)
    s1 = s0 + HALF
    m0 = (pos <= s0) if reverse else (pos >= s0)
    m1 = (pos <= s1) if reverse else (pos >= s1)
    lo64 = lax.broadcasted_iota(jnp.int32, (HALF, LANES), 1) < HALF
    zero = jnp.zeros((HALF, LANES), BF16)

    for g in range(N_GROUPS):
        gs = slice(g * GROUP_W, (g + 1) * GROUP_W)
        cg = c_ref[0, :, g * D_STATE:(g + 1) * D_STATE]
        bg = b_ref[0, :, g * D_STATE:(g + 1) * D_STATE]
        cb0 = cb_ref[g, 0]
        cb1 = cb_ref[g, 1]
        hg = h_ref[0, :, gs]
        yoff = jnp.dot(cg, hg.astype(BF16), preferred_element_type=F32)
        for p in range(GROUP_W // LANES):
            k = g * (GROUP_W // LANES) + p
            sl = slice(k * LANES, (k + 1) * LANES)
            a = ecs_ref[:, sl]
            t0 = jnp.exp2(jnp.where(m0, a - row_ref[0, k:k + 1, :], NEG_BIG)) * cb0
            t1 = jnp.exp2(jnp.where(m1, a - row_ref[1, k:k + 1, :], NEG_BIG)) * cb1
            lhs = jnp.concatenate([t0, t1], axis=1).astype(BF16)
            xa = xdt_ref[0:HALF, sl]
            xb = xdt_ref[HALF:q, sl]
            rhs = jnp.concatenate([jnp.where(lo64, xa, zero), jnp.where(lo64, zero, xa),
                                   jnp.where(lo64, xb, zero), jnp.where(lo64, zero, xb)], axis=0)
            yd = jnp.dot(lhs, rhs, preferred_element_type=F32)
            y = yd + yoff[:, p * LANES:(p + 1) * LANES] * jnp.exp2(a)
            y_ref[0, :, sl] = y.astype(y_ref.dtype)
        st = lax.dot_general(bg, xw_ref[:, gs], (((0,), (0,)), ((), ())), preferred_element_type=F32)
        h_ref[0, :, gs] = hg * jnp.exp2(ecs_ref[last:last + 1, gs]) + st


def _ssd_kernel(xsf_ref, bf_ref, cf_ref, dtf_ref, daf_ref, xsb_ref, bb_ref, cbk_ref, dtb_ref, dab_ref,
                r3_ref, r2_ref, h0f_ref, h0b_ref, yf_ref, yb_ref, hf_ref, hb_ref,
                ecs_ref, xdt_ref, xw_ref, row_ref, cb_ref):
    @pl.when(pl.program_id(1) == 0)
    def _():
        hf_ref[...] = h0f_ref[...]
        hb_ref[...] = h0b_ref[...]

    sc = lambda d: (ecs_ref.at[d], xdt_ref.at[d], xw_ref.at[d], row_ref.at[d], cb_ref.at[d])
    _ssd_prologue(xsf_ref, bf_ref, cf_ref, dtf_ref, daf_ref, r3_ref, r2_ref, *sc(0), reverse=False)
    _ssd_prologue(xsb_ref, bb_ref, cbk_ref, dtb_ref, dab_ref, r3_ref, r2_ref, *sc(1), reverse=True)
    _ssd_main(bf_ref, cf_ref, yf_ref, hf_ref, *sc(0), reverse=False)
    _ssd_main(bb_ref, cbk_ref, yb_ref, hb_ref, *sc(1), reverse=True)


def _ssd(xbc, dt, da, r3, r2, h0f, h0b):
    bsz, seq, _ = xbc.shape
    nc = seq // CHUNK
    q = CHUNK
    bcol = D_INNER // BC_DIM
    fwd = lambda c: c
    bwd = lambda c: nc - 1 - c
    const = lambda r, c: pl.BlockSpec((r, c), lambda b, cc: (0, 0), pipeline_mode=pl.Buffered(1))
    state = pl.BlockSpec((1, D_STATE, D_INNER), lambda b, c: (b, 0, 0))

    def chunk_specs(ci, dcol):
        return [pl.BlockSpec((1, q, D_INNER), lambda b, c: (b, ci(c), 0)),
                pl.BlockSpec((1, q, BC_DIM), lambda b, c: (b, ci(c), bcol)),
                pl.BlockSpec((1, q, BC_DIM), lambda b, c: (b, ci(c), bcol + 1)),
                pl.BlockSpec((1, q, LANES), lambda b, c: (b, ci(c), dcol)),
                pl.BlockSpec((1, q, LANES), lambda b, c: (b, ci(c), dcol))]

    return pl.pallas_call(
        _ssd_kernel,
        grid=(bsz, nc),
        in_specs=chunk_specs(fwd, 0) + chunk_specs(bwd, 1)
        + [const(2 * LANES, D_INNER), const(2 * LANES, D_INNER), state, state],
        out_specs=(pl.BlockSpec((1, q, D_INNER), lambda b, c: (b, fwd(c), 0)),
                   pl.BlockSpec((1, q, D_INNER), lambda b, c: (b, bwd(c), 0)),
                   state, state),
        out_shape=(jax.ShapeDtypeStruct((bsz, seq, D_INNER), BF16),
                   jax.ShapeDtypeStruct((bsz, seq, D_INNER), BF16),
                   jax.ShapeDtypeStruct((bsz, D_STATE, D_INNER), F32),
                   jax.ShapeDtypeStruct((bsz, D_STATE, D_INNER), F32)),
        scratch_shapes=[pltpu.VMEM((2, q, D_INNER), F32),
                        pltpu.VMEM((2, q, D_INNER), BF16),
                        pltpu.VMEM((2, q, D_INNER), BF16),
                        pltpu.VMEM((2, 2, N_HEADS // 2, LANES), F32),
                        pltpu.VMEM((2, N_GROUPS, 2, q, LANES), F32)],
        compiler_params=_cparams("arbitrary", "arbitrary"),
        name="ssd_scan",
    )(xbc, xbc, xbc, dt, da, xbc, xbc, xbc, dt, da, r3, r2, h0f, h0b)


def _gate_proja_kernel(yf_ref, yb_ref, xs_ref, sz_ref, dsk_ref, nw_ref, w_ref, o_ref):
    y = (yf_ref[...].astype(F32) + yb_ref[...].astype(F32)
         + dsk_ref[...] * xs_ref[...].astype(F32))
    u = y * sz_ref[...].astype(F32)
    un = _rms(u, nw_ref[...]).astype(BF16)
    o_ref[...] = jnp.dot(un, w_ref[...], preferred_element_type=F32).astype(o_ref.dtype)


def _gate_proja(yf, yb, xbc, sz, dsk, nw, w, tm=256):
    t = yf.shape[0]
    tok = pl.BlockSpec((tm, D_INNER), lambda i: (i, 0))
    vec = pl.BlockSpec((1, D_INNER), lambda i: (0, 0))
    return pl.pallas_call(
        _gate_proja_kernel,
        grid=(t // tm,),
        in_specs=[tok, tok, tok, tok, vec, vec,
                  pl.BlockSpec((D_INNER, D_MODEL), lambda i: (0, 0), pipeline_mode=pl.Buffered(1))],
        out_specs=pl.BlockSpec((tm, D_MODEL), lambda i: (i, 0)),
        out_shape=jax.ShapeDtypeStruct((t, D_MODEL), BF16),
        compiler_params=_cparams("arbitrary"),
        name="gate_proj_a",
    )(yf, yb, xbc, sz, dsk, nw, w)


CF_TT = 2048
CF_HALO = 1024
CF_CB = 256


def _cfconv_kernel(gp_ref, gc_ref, gn_ref, w_ref, b_ref, o_ref, win_ref):
    r = pl.program_id(2)
    n = pl.num_programs(2)
    win_ref[0:CF_HALO, :] = jnp.where(r > 0, gp_ref[0].astype(F32), 0.0)
    win_ref[CF_HALO:CF_HALO + CF_TT, :] = gc_ref[0].astype(F32)
    win_ref[CF_HALO + CF_TT:2 * CF_HALO + CF_TT, :] = jnp.where(r < n - 1, gn_ref[0].astype(F32), 0.0)
    half = CF_KERNEL // 2

    def body(rr, carry):
        base = pl.multiple_of(CF_HALO + 2 * rr * GRID_W - half * GRID_W, GRID_W)
        acc0 = jnp.zeros((GRID_W, CF_CB), F32) + b_ref[...]
        acc1 = acc0
        for j in range(CF_KERNEL + 1):
            xrow = win_ref[pl.ds(base + j * GRID_W, GRID_W), :]
            if j < CF_KERNEL:
                acc0 = acc0 + w_ref[j:j + 1, :] * xrow
            if j > 0:
                acc1 = acc1 + w_ref[j - 1:j, :] * xrow
        out0 = pl.multiple_of(2 * rr * GRID_W, GRID_W)
        o_ref[0, pl.ds(out0, GRID_W), :] = acc0.astype(o_ref.dtype)
        o_ref[0, pl.ds(pl.multiple_of(out0 + GRID_W, GRID_W), GRID_W), :] = acc1.astype(o_ref.dtype)
        return carry

    lax.fori_loop(0, CF_TT // (2 * GRID_W), body, 0)


def _cfconv(g, w, b):
    bsz, seq, ch = g.shape
    nt = seq // CF_TT
    hpt = CF_TT // CF_HALO
    nh = seq // CF_HALO
    blk = lambda f: pl.BlockSpec((1, CF_TT, CF_CB), f)
    hblk = lambda f: pl.BlockSpec((1, CF_HALO, CF_CB), f)
    return pl.pallas_call(
        _cfconv_kernel,
        grid=(bsz, ch // CF_CB, nt),
        in_specs=[hblk(lambda bb, c, r: (bb, jnp.maximum(r * hpt - 1, 0), c)),
                  blk(lambda bb, c, r: (bb, r, c)),
                  hblk(lambda bb, c, r: (bb, jnp.minimum((r + 1) * hpt, nh - 1), c)),
                  pl.BlockSpec((32, CF_CB), lambda bb, c, r: (0, c)),
                  pl.BlockSpec((1, CF_CB), lambda bb, c, r: (0, c))],
        out_specs=blk(lambda bb, c, r: (bb, r, c)),
        out_shape=jax.ShapeDtypeStruct((bsz, seq, ch), BF16),
        scratch_shapes=[pltpu.VMEM((CF_TT + 2 * CF_HALO, CF_CB), F32)],
        compiler_params=_cparams("arbitrary", "arbitrary", "arbitrary"),
        name="conformer_conv",
    )(g, g, g, w, b)


def _merge_kernel(cf_ref, ya_ref, sg_ref, x_ref, lng_ref, lnb_ref, wpb_ref, wout_ref, g1_ref,
                  nw_ref, sc_ref, sh_ref, o_ref, hx_ref):
    cf = cf_ref[...].astype(F32)
    mu = jnp.mean(cf, axis=-1, keepdims=True)
    xc = cf - mu
    var = jnp.mean(xc * xc, axis=-1, keepdims=True)
    ln = xc * lax.rsqrt(var + EPS) * lng_ref[...] + lnb_ref[...]
    yb = jnp.dot(_silu(ln).astype(BF16), wpb_ref[...], preferred_element_type=F32)
    m = (sg_ref[:, 0:D_MODEL].astype(F32) * ya_ref[...].astype(F32)
         + sg_ref[:, D_MODEL:2 * D_MODEL].astype(F32) * yb)
    mix = jnp.dot(m.astype(BF16), wout_ref[...], preferred_element_type=F32)
    x1 = x_ref[...] + g1_ref[0] * mix
    o_ref[...] = x1
    hx_ref[...] = (_rms(x1, nw_ref[...]) * (1.0 + sc_ref[0]) + sh_ref[0]).astype(hx_ref.dtype)


def _merge(cf, ya, sg, x2d, lng, lnb, wpb, wout, g1, nw, sc2, sh2, tokens_per_batch, tm=256):
    t = x2d.shape[0]
    tpb = tokens_per_batch // tm
    tok = lambda w: pl.BlockSpec((tm, w), lambda i: (i, 0))
    vec = pl.BlockSpec((1, D_MODEL), lambda i: (0, 0))
    mod = pl.BlockSpec((1, 1, D_MODEL), lambda i: (i // tpb, 0, 0))
    wspec = pl.BlockSpec((D_MODEL, D_MODEL), lambda i: (0, 0), pipeline_mode=pl.Buffered(1))
    return pl.pallas_call(
        _merge_kernel,
        grid=(t // tm,),
        in_specs=[tok(D_MODEL), tok(D_MODEL), tok(2 * D_MODEL), tok(D_MODEL), vec, vec, wspec, wspec,
                  mod, vec, mod, mod],
        out_specs=(tok(D_MODEL), tok(D_MODEL)),
        out_shape=(jax.ShapeDtypeStruct((t, D_MODEL), F32), jax.ShapeDtypeStruct((t, D_MODEL), BF16)),
        compiler_params=_cparams("arbitrary"),
        name="merge_out_proj",
    )(cf, ya, sg, x2d, lng, lnb, wpb, wout, g1, nw, sc2, sh2)


def _ffn_up_kernel(a_ref, wg_ref, wu_ref, o_ref, wgbf_ref, wubf_ref):
    _cast_weight_once(wg_ref, wgbf_ref)
    _cast_weight_once(wu_ref, wubf_ref)
    a = a_ref[...]
    gate = jnp.dot(a, wgbf_ref[...], preferred_element_type=F32)
    up = jnp.dot(a, wubf_ref[...], preferred_element_type=F32)
    o_ref[...] = (_silu(gate) * up).astype(o_ref.dtype)


def _ffn_up(hx, wg, wu, tm=1024, tn=512):
    t, k = hx.shape
    n = wg.shape[1]
    return pl.pallas_call(
        _ffn_up_kernel,
        grid=(n // tn, t // tm),
        in_specs=[pl.BlockSpec((tm, k), lambda j, i: (i, 0)),
                  pl.BlockSpec((k, tn), lambda j, i: (0, j)),
                  pl.BlockSpec((k, tn), lambda j, i: (0, j))],
        out_specs=pl.BlockSpec((tm, tn), lambda j, i: (i, j)),
        out_shape=jax.ShapeDtypeStruct((t, n), BF16),
        scratch_shapes=[pltpu.VMEM((k, tn), BF16), pltpu.VMEM((k, tn), BF16)],
        compiler_params=_cparams("arbitrary", "arbitrary"),
        name="ffn_up",
    )(hx, wg, wu)


def _ffn_down_kernel(h_ref, w_ref, x_ref, g2_ref, nf_ref, o_ref):
    acc = jnp.dot(h_ref[...], w_ref[...], preferred_element_type=F32)
    o_ref[...] = _rms(x_ref[...] + g2_ref[0] * acc, nf_ref[...])


def _ffn_down(h1, wd, x1, g2, nf, tokens_per_batch, tm=256):
    t = x1.shape[0]
    tpb = tokens_per_batch // tm
    return pl.pallas_call(
        _ffn_down_kernel,
        grid=(t // tm,),
        in_specs=[pl.BlockSpec((tm, D_FF), lambda i: (i, 0)),
                  pl.BlockSpec((D_FF, D_MODEL), lambda i: (0, 0), pipeline_mode=pl.Buffered(1)),
                  pl.BlockSpec((tm, D_MODEL), lambda i: (i, 0)),
                  pl.BlockSpec((1, 1, D_MODEL), lambda i: (i // tpb, 0, 0)),
                  pl.BlockSpec((1, D_MODEL), lambda i: (0, 0))],
        out_specs=pl.BlockSpec((tm, D_MODEL), lambda i: (i, 0)),
        out_shape=jax.ShapeDtypeStruct((t, D_MODEL), F32),
        compiler_params=_cparams("arbitrary"),
        name="ffn_down",
    )(h1, wd, x1, g2, nf)


_HEAD_PERM = np.concatenate([np.arange(0, N_HEADS, 2), np.arange(1, N_HEADS, 2)])


def _expand_matrix(row_blocks):
    col_head = np.arange(D_INNER) // HEAD_DIM
    pat = (_HEAD_PERM[:, None] == col_head[None, :]).astype(np.float32)
    zero = np.zeros_like(pat)
    return jnp.asarray(np.concatenate([pat if on else zero for on in row_blocks], axis=0), BF16)


def _front(h3, dt, da, bsz, seq, w_in, cw, cb, r3, r2, h0f, h0b):
    xbc = _mm_xbc(h3, w_in, cw, cb).reshape(bsz, seq, CONV_DIM)
    dt = dt.reshape(bsz, seq, 2 * LANES)
    da = da.reshape(bsz, seq, 2 * LANES)
    yf, yb, hf, hb = _ssd(xbc, dt, da, r3, r2, h0f, h0b)
    return xbc, yf, yb, hf, hb


@jax.jit
def kernel(x, c, ctx, c_ctx, w_mod, b_mod, norm_mix, w_in, ssm_conv_w, ssm_conv_b, dt_bias, a_log,
           d_skip, ssm_norm, cf_conv_w, cf_conv_b, cf_ln_g, cf_ln_b, w_proj_a, w_proj_b, w_out,
           norm_ffn, w_ffn_gate, w_ffn_up, w_ffn_down, norm_final):
    bsz, seq, _ = x.shape
    ctx_len = ctx.shape[1]
    t = bsz * seq

    cc = jnp.zeros((8, D_MODEL), F32).at[0:bsz].set(c).at[bsz].set(c_ctx)
    mod = _ada(cc, w_mod[0], b_mod[0])
    sh1, sc1, g1, sh2, sc2, g2 = [m.reshape(8, 1, D_MODEL) for m in jnp.split(mod, 6, axis=-1)]
    lat = lambda m: m[0:bsz]
    cx = lambda m: jnp.broadcast_to(m[bsz:bsz + 1], (bsz, 1, D_MODEL))

    w = w_in[0]
    wdt = w[:, OFF_XBC:OFF_DT]
    wdt_f = wdt[:, 0:N_HEADS][:, _HEAD_PERM]
    wdt_b = wdt[:, N_HEADS:2 * N_HEADS][:, _HEAD_PERM]
    wdt4 = jnp.concatenate([wdt_f, wdt_b, wdt_b, wdt_f], axis=1).astype(BF16)
    wglu_u = w[:, OFF_DT:OFF_DT + D_MODEL].astype(BF16)
    wglu_v = w[:, OFF_DT + D_MODEL:OFF_GLU].astype(BF16)
    wgates = w[:, OFF_GLU:].astype(BF16)
    perm2 = lambda p: jnp.concatenate([p[0][_HEAD_PERM], p[1][_HEAD_PERM],
                                       p[1][_HEAD_PERM], p[0][_HEAD_PERM]]).reshape(1, 2 * LANES)
    dtb4 = perm2(dt_bias[0])
    alog4 = perm2(a_log[0])
    cw = jnp.zeros((8, CONV_DIM), F32).at[0:SSM_CONV].set(ssm_conv_w[0])
    cb = ssm_conv_b[0].reshape(1, CONV_DIM)
    r3 = _expand_matrix([True, True, True, False])
    r2 = _expand_matrix([True, False, True, False])
    dsk = jnp.repeat(d_skip[0], HEAD_DIM).reshape(1, D_INNER)
    cfw = jnp.zeros((32, D_MODEL), F32).at[0:CF_KERNEL].set(cf_conv_w[0])

    hc, dtc, dac = _normmod(ctx, norm_mix[0], cx(sc1), cx(sh1), wdt4, dtb4, alog4, tm=ctx_len)
    zero_state = jnp.zeros((bsz, D_STATE, D_INNER), F32)
    _, _, _, hf, hb = _front(hc, dtc, dac, bsz, ctx_len, w, cw, cb, r3, r2, zero_state, zero_state)

    x2d = x.reshape(t, D_MODEL)
    hx, dtx, dax = _normmod(x, norm_mix[0], lat(sc1), lat(sh1), wdt4, dtb4, alog4, tm=1024)
    xbc, yf, yb, _, _ = _front(hx, dtx, dax, bsz, seq, w, cw, cb, r3, r2, hf, hb)
    sz, sg = _mm_zg(hx, w, wgates)
    glu = _mm_glu(hx, wglu_u, wglu_v)

    ya = _gate_proja(yf.reshape(t, D_INNER), yb.reshape(t, D_INNER), xbc.reshape(t, CONV_DIM), sz,
                     dsk, ssm_norm[0].reshape(1, D_INNER), w_proj_a[0].astype(BF16))
    cf = _cfconv(glu.reshape(bsz, seq, D_MODEL), cfw, cf_conv_b[0].reshape(1, D_MODEL))
    x1, hx2 = _merge(cf.reshape(t, D_MODEL), ya, sg, x2d, cf_ln_g[0].reshape(1, D_MODEL),
                     cf_ln_b[0].reshape(1, D_MODEL), w_proj_b[0].astype(BF16), w_out[0].astype(BF16),
                     lat(g1), norm_ffn[0].reshape(1, D_MODEL), lat(sc2), lat(sh2), seq)
    h1 = _ffn_up(hx2, w_ffn_gate[0], w_ffn_up[0])
    out = _ffn_down(h1, w_ffn_down[0].astype(BF16), x1, lat(g2), norm_final.reshape(1, D_MODEL), seq)
    return out.reshape(bsz, seq, D_MODEL)
```
